```python
import jax, jax.numpy as jnp
from jax import lax
import numpy as np

D_MODEL = 1024
BATCH = 4
SEQ = 4096
DEPTH = 4

N_MIXERS = 3
N_FOX = (DEPTH + 2) // 3
N_MOBA = (DEPTH + 1) // 3
N_RET = DEPTH // 3

FOX_HEADS = 8
FOX_HD = D_MODEL // FOX_HEADS
FOX_QBLOCK = 128
FOX_IN = 3 * D_MODEL + FOX_HEADS

MOBA_HEADS = 8
MOBA_HD = D_MODEL // MOBA_HEADS
MOBA_BLOCK = 256
MOBA_TOPK = 3
MOBA_QCHUNK = 16
MOBA_IN = 3 * D_MODEL

RET_HEADS = 4
RET_DK = D_MODEL // RET_HEADS
RET_DV = 2 * RET_DK
RET_QK = RET_HEADS * RET_DK
RET_V = RET_HEADS * RET_DV
RET_IN = 2 * RET_QK + 2 * RET_V
RET_CHUNK = 128
RET_ROPE_BASE = 10000.0
RET_GN_EPS = 1e-6

N_EXPERTS = 32
TOP_K = 4
D_FF = D_MODEL
SWIGLU_LIMIT = 7.0
SWIGLU_ALPHA = 1.702
MOE_BLOCK = 256

LN_EPS = 1e-5
DEEPNORM_ALPHA = (2 * DEPTH) ** 0.25
DEEPNORM_BETA = (8 * DEPTH) ** -0.25

kernel_name = 'fox_moba_retnet_moe_deepnorm_trunk'


def layer_norm(x, g, b):
    xf = x.astype(jnp.float32)
    mu = jnp.mean(xf, axis=-1, keepdims=True)
    var = jnp.mean(jnp.square(xf - mu), axis=-1, keepdims=True)
    return ((xf - mu) * lax.rsqrt(var + LN_EPS) * g + b).astype(x.dtype)


def _heads(t, n_heads, hd):
    B, S, _ = t.shape
    return t.reshape(B, S, n_heads, hd).transpose(0, 2, 1, 3)


def forgetting_attention(x, w_in, b_f, w_out):
    B, S, D = x.shape
    proj = x @ w_in
    q = _heads(proj[..., :D], FOX_HEADS, FOX_HD)
    k = _heads(proj[..., D:2 * D], FOX_HEADS, FOX_HD)
    v = _heads(proj[..., 2 * D:3 * D], FOX_HEADS, FOX_HD)
    log_f = jax.nn.log_sigmoid((proj[..., 3 * D:] + b_f).astype(jnp.float32))
    c = jnp.cumsum(log_f, axis=1).transpose(0, 2, 1)
    nq = S // FOX_QBLOCK
    q_blocks = q.reshape(B, FOX_HEADS, nq, FOX_QBLOCK, FOX_HD).transpose(2, 0, 1, 3, 4)
    c_blocks = c.reshape(B, FOX_HEADS, nq, FOX_QBLOCK).transpose(2, 0, 1, 3)
    k_pos = jnp.arange(S)
    scale = FOX_HD ** -0.5

    def block(args):
        qi, ci, i = args
        s = jnp.einsum('bhqd,bhkd->bhqk', qi, k).astype(jnp.float32) * scale
        s = s + ci[..., None] - c[:, :, None, :]
        q_pos = i * FOX_QBLOCK + jnp.arange(FOX_QBLOCK)
        s = jnp.where(k_pos[None, :] <= q_pos[:, None], s, -jnp.inf)
        p = jax.nn.softmax(s, axis=-1).astype(v.dtype)
        return jnp.einsum('bhqk,bhkd->bhqd', p, v)

    o = lax.map(block, (q_blocks, c_blocks, jnp.arange(nq)))
    o = o.transpose(1, 0, 3, 2, 4).reshape(B, S, D)
    return o @ w_out


def moba_attention(x, w_in, w_out):
    B, S, D = x.shape
    proj = x @ w_in
    q = _heads(proj[..., :D], MOBA_HEADS, MOBA_HD)
    k = _heads(proj[..., D:2 * D], MOBA_HEADS, MOBA_HD)
    v = _heads(proj[..., 2 * D:], MOBA_HEADS, MOBA_HD)
    nb = -(-S // MOBA_BLOCK)
    pad = nb * MOBA_BLOCK - S
    k_blk = jnp.pad(k, ((0, 0), (0, 0), (0, pad), (0, 0))).reshape(B, MOBA_HEADS, nb, MOBA_BLOCK, MOBA_HD)
    v_blk = jnp.pad(v, ((0, 0), (0, 0), (0, pad), (0, 0))).reshape(B, MOBA_HEADS, nb, MOBA_BLOCK, MOBA_HD)
    k_mean = jnp.mean(k_blk.astype(jnp.float32), axis=3)
    topk = min(MOBA_TOPK, nb)
    nqc = S // MOBA_QCHUNK
    q_chunks = q.reshape(B, MOBA_HEADS, nqc, MOBA_QCHUNK, MOBA_HD).transpose(2, 0, 1, 3, 4)
    b_ix = jnp.arange(B)[:, None, None, None]
    h_ix = jnp.arange(MOBA_HEADS)[None, :, None, None]
    blk_ids = jnp.arange(nb)
    scale = MOBA_HD ** -0.5

    def chunk(args):
        qi, i = args
        q_pos = i * MOBA_QCHUNK + jnp.arange(MOBA_QCHUNK)
        own = (i * MOBA_QCHUNK) // MOBA_BLOCK
        gate = jnp.einsum('bhqd,bhnd->bhqn', qi.astype(jnp.float32), k_mean)
        gate = jnp.where(blk_ids < own, gate, -jnp.inf)
        _, sel = lax.top_k(gate, topk)
        valid = jnp.arange(topk) < own
        k_sel = k_blk[b_ix, h_ix, sel]
        v_sel = v_blk[b_ix, h_ix, sel]
        s_sel = jnp.einsum('bhqd,bhqnkd->bhqnk', qi, k_sel).astype(jnp.float32) * scale
        s_sel = jnp.where(valid[:, None], s_sel, -jnp.inf).reshape(B, MOBA_HEADS, MOBA_QCHUNK, topk * MOBA_BLOCK)
        k_own = lax.dynamic_index_in_dim(k_blk, own, axis=2, keepdims=False)
        v_own = lax.dynamic_index_in_dim(v_blk, own, axis=2, keepdims=False)
        s_own = jnp.einsum('bhqd,bhkd->bhqk', qi, k_own).astype(jnp.float32) * scale
        own_pos = own * MOBA_BLOCK + jnp.arange(MOBA_BLOCK)
        s_own = jnp.where(own_pos[None, :] <= q_pos[:, None], s_own, -jnp.inf)
        p = jax.nn.softmax(jnp.concatenate([s_sel, s_own], axis=-1), axis=-1).astype(v.dtype)
        p_sel = p[..., :topk * MOBA_BLOCK].reshape(B, MOBA_HEADS, MOBA_QCHUNK, topk, MOBA_BLOCK)
        p_own = p[..., topk * MOBA_BLOCK:]
        return (jnp.einsum('bhqnk,bhqnkd->bhqd', p_sel, v_sel)
                + jnp.einsum('bhqk,bhkd->bhqd', p_own, v_own))

    o = lax.map(chunk, (q_chunks, jnp.arange(nqc)))
    o = o.transpose(1, 0, 3, 2, 4).reshape(B, S, D)
    return o @ w_out


def _rotate(t, cos, sin):
    t1, t2 = jnp.split(t, 2, axis=-1)
    return jnp.concatenate([t1 * cos - t2 * sin, t1 * sin + t2 * cos], axis=-1)


def retention(x, w_in, gn_g, w_out):
    B, S, _ = x.shape
    proj = x @ w_in
    f32 = jnp.float32
    q = _heads(proj[..., :RET_QK], RET_HEADS, RET_DK).astype(f32)
    k = _heads(proj[..., RET_QK:2 * RET_QK], RET_HEADS, RET_DK).astype(f32) * (RET_DK ** -0.5)
    v = _heads(proj[..., 2 * RET_QK:2 * RET_QK + RET_V], RET_HEADS, RET_DV).astype(f32)
    g = proj[..., 2 * RET_QK + RET_V:]
    pos = jnp.arange(S, dtype=f32)
    inv_freq = jnp.exp(-jnp.log(RET_ROPE_BASE) * jnp.arange(0, RET_DK, 2, dtype=f32) / RET_DK)
    ang = pos[:, None] * inv_freq[None, :]
    cos, sin = jnp.cos(ang), jnp.sin(ang)
    q = _rotate(q, cos, sin)
    k = _rotate(k, cos, sin)
    log_gamma = jnp.log(1.0 - jnp.exp2(-5.0 - jnp.arange(RET_HEADS, dtype=f32)))
    n = jnp.arange(RET_CHUNK, dtype=f32)
    diff = n[:, None] - n[None, :]
    d_mask = jnp.where(diff[None] >= 0, jnp.exp(diff[None] * log_gamma[:, None, None]), 0.0)
    xi = jnp.exp((n[None, :] + 1.0) * log_gamma[:, None])
    zeta = jnp.exp((RET_CHUNK - 1.0 - n[None, :]) * log_gamma[:, None])
    g_chunk = jnp.exp(RET_CHUNK * log_gamma)
    nc = S // RET_CHUNK

    def to_chunks(t):
        return t.reshape(B, RET_HEADS, nc, RET_CHUNK, t.shape[-1]).transpose(2, 0, 1, 3, 4)

    def step(R, inp):
        qc, kc, vc = inp
        s = jnp.einsum('bhnd,bhmd->bhnm', qc, kc) * d_mask[None]
        inner = jnp.einsum('bhnm,bhmv->bhnv', s, vc)
        cross = jnp.einsum('bhnd,bhdv->bhnv', qc, R) * xi[None, :, :, None]
        R = g_chunk[None, :, None, None] * R + jnp.einsum('bhmd,bhmv->bhdv', kc, vc * zeta[None, :, :, None])
        return R, inner + cross

    R0 = jnp.zeros((B, RET_HEADS, RET_DK, RET_DV), f32)
    _, o = lax.scan(step, R0, (to_chunks(q), to_chunks(k), to_chunks(v)))
    o = o.transpose(1, 0, 3, 2, 4).reshape(B, S, RET_HEADS, RET_DV)
    mu = jnp.mean(o, axis=-1, keepdims=True)
    var = jnp.mean(jnp.square(o - mu), axis=-1, keepdims=True)
    o = ((o - mu) * lax.rsqrt(var + RET_GN_EPS)).reshape(B, S, RET_V) * gn_g
    o = (jax.nn.silu(g.astype(f32)) * o).astype(x.dtype)
    return o @ w_out


def clamped_swiglu(h):
    glu, lin = h[..., :D_FF], h[..., D_FF:]
    glu = jnp.minimum(glu, SWIGLU_LIMIT)
    lin = jnp.clip(lin, -SWIGLU_LIMIT, SWIGLU_LIMIT)
    return glu * jax.nn.sigmoid(SWIGLU_ALPHA * glu) * (lin + 1.0)


def moe(x2, router_w, router_b, w1, b1, w2, b2):
    T, D = x2.shape
    logits = (x2 @ router_w + router_b).astype(jnp.float32)
    top_v, top_i = lax.top_k(logits, TOP_K)
    gates = jax.nn.softmax(top_v, axis=-1).astype(x2.dtype)
    A = T * TOP_K
    e_flat = top_i.reshape(-1)
    tok_flat = jnp.arange(A, dtype=jnp.int32) // TOP_K
    g_flat = gates.reshape(-1)
    order = jnp.argsort(e_flat)
    e_s, tok_s, g_s = e_flat[order], tok_flat[order], g_flat[order]
    counts = jnp.bincount(e_flat, length=N_EXPERTS)
    starts = jnp.cumsum(counts) - counts
    padded = ((counts + MOE_BLOCK - 1) // MOE_BLOCK) * MOE_BLOCK
    pad_ends = jnp.cumsum(padded)
    pad_starts = pad_ends - padded
    dest = pad_starts[e_s] + (jnp.arange(A, dtype=jnp.int32) - starts[e_s])
    n_blocks = -(-A // MOE_BLOCK) + N_EXPERTS
    P = n_blocks * MOE_BLOCK
    buf_tok = jnp.full((P,), T, jnp.int32).at[dest].set(tok_s)
    buf_g = jnp.zeros((P,), x2.dtype).at[dest].set(g_s)
    blk_start = jnp.arange(n_blocks, dtype=jnp.int32) * MOE_BLOCK
    blk_e = jnp.minimum(jnp.searchsorted(pad_ends, blk_start, side='right'), N_EXPERTS - 1)
    x_pad = jnp.concatenate([x2, jnp.zeros((1, D), x2.dtype)], axis=0)

    def expert_block(args):
        tok, g, e = args
        h = x_pad[tok] @ w1[e] + b1[e]
        y = clamped_swiglu(h) @ w2[e] + b2[e]
        return y * g[:, None]

    y = lax.map(expert_block, (buf_tok.reshape(n_blocks, MOE_BLOCK),
                               buf_g.reshape(n_blocks, MOE_BLOCK), blk_e))
    out = jnp.zeros((T + 1, D), x2.dtype).at[buf_tok].add(y.reshape(P, D))
    return out[:T]


def setup_inputs(seed: int = 0) -> dict:
    key = jax.random.key(seed)
    ks = jax.random.split(key, 20)
    nrm = lambda k, shape: jax.random.normal(k, shape, jnp.float32)
    D = D_MODEL
    x = nrm(ks[0], (BATCH, SEQ, D))
    fox_w_in = nrm(ks[1], (N_FOX, D, FOX_IN)) * D ** -0.5
    fox_b_f = 2.0 + 0.5 * nrm(ks[2], (N_FOX, FOX_HEADS))
    fox_w_out = nrm(ks[3], (N_FOX, D, D)) * (D ** -0.5 * DEEPNORM_BETA)
    moba_w_in = nrm(ks[4], (N_MOBA, D, MOBA_IN)) * D ** -0.5
    moba_w_out = nrm(ks[5], (N_MOBA, D, D)) * (D ** -0.5 * DEEPNORM_BETA)
    ret_w_in = nrm(ks[6], (N_RET, D, RET_IN)) * D ** -0.5
    ret_gn_g = 1.0 + 0.02 * nrm(ks[7], (N_RET, RET_V))
    ret_w_out = nrm(ks[8], (N_RET, RET_V, D)) * (RET_V ** -0.5 * DEEPNORM_BETA)
    ln_g = 1.0 + 0.02 * nrm(ks[9], (DEPTH, 2, D))
    ln_b = 0.02 * nrm(ks[10], (DEPTH, 2, D))
    router_w = nrm(ks[11], (DEPTH, D, N_EXPERTS)) * D ** -0.5
    router_b = 0.01 * nrm(ks[12], (DEPTH, N_EXPERTS))
    moe_w1 = nrm(ks[13], (DEPTH, N_EXPERTS, D, 2 * D_FF)) * D ** -0.5
    moe_b1 = 0.01 * nrm(ks[14], (DEPTH, N_EXPERTS, 2 * D_FF))
    moe_w2 = nrm(ks[15], (DEPTH, N_EXPERTS, D_FF, D)) * (D_FF ** -0.5 * DEEPNORM_BETA)
    moe_b2 = 0.01 * nrm(ks[16], (DEPTH, N_EXPERTS, D))
    return {'x': x, 'fox_w_in': fox_w_in, 'fox_b_f': fox_b_f, 'fox_w_out': fox_w_out,
            'moba_w_in': moba_w_in, 'moba_w_out': moba_w_out,
            'ret_w_in': ret_w_in, 'ret_gn_g': ret_gn_g, 'ret_w_out': ret_w_out,
            'ln_g': ln_g, 'ln_b': ln_b, 'router_w': router_w, 'router_b': router_b,
            'moe_w1': moe_w1, 'moe_b1': moe_b1, 'moe_w2': moe_w2, 'moe_b2': moe_b2}


def reference(x, fox_w_in, fox_b_f, fox_w_out, moba_w_in, moba_w_out, ret_w_in, ret_gn_g,
              ret_w_out, ln_g, ln_b, router_w, router_b, moe_w1, moe_b1, moe_w2, moe_b2):
    B, S, D = x.shape
    for i in range(DEPTH):
        kind = i % N_MIXERS
        j = i // N_MIXERS
        if kind == 0:
            h = forgetting_attention(x, fox_w_in[j], fox_b_f[j], fox_w_out[j])
        elif kind == 1:
            h = moba_attention(x, moba_w_in[j], moba_w_out[j])
        else:
            h = retention(x, ret_w_in[j], ret_gn_g[j], ret_w_out[j])
        x = layer_norm(DEEPNORM_ALPHA * x + h, ln_g[i, 0], ln_b[i, 0])
        y = moe(x.reshape(B * S, D), router_w[i], router_b[i], moe_w1[i], moe_b1[i],
                moe_w2[i], moe_b2[i]).reshape(B, S, D)
        x = layer_norm(DEEPNORM_ALPHA * x + y, ln_g[i, 1], ln_b[i, 1])
    return x
```

```python
import functools

import jax
import jax.numpy as jnp
from jax import lax
from jax.experimental import pallas as pl
from jax.experimental.pallas import tpu as pltpu

F32, BF16, I32 = jnp.float32, jnp.bfloat16, jnp.int32
_HIGHEST = lax.Precision.HIGHEST
_NT = (((1,), (1,)), ((), ()))
_LANES = 128
_MASKED = -1e30
_VMEM_LIMIT = 56 * 1024 * 1024

DEPTH = 4
FOX_HEADS, MOBA_HEADS, RET_HEADS = 8, 8, 4
MOBA_BLOCK, MOBA_TOPK = 256, 3
RET_ROPE_BASE, RET_GN_EPS = 10000.0, 1e-6
N_EXPERTS, TOP_K, MOE_BLOCK = 32, 4, 256
SWIGLU_LIMIT, SWIGLU_ALPHA = 7.0, 1.702
LN_EPS = 1e-5
DEEPNORM_ALPHA = (2 * DEPTH) ** 0.25


def _params(*sem):
    return pltpu.CompilerParams(dimension_semantics=sem, vmem_limit_bytes=_VMEM_LIMIT)


def _layer_norm(y, g, b):
    mu = jnp.mean(y, axis=-1, keepdims=True)
    d = y - mu
    var = jnp.mean(d * d, axis=-1, keepdims=True)
    return d * lax.rsqrt(var + LN_EPS) * g + b


def _mm_kernel(a_ref, w_ref, o_ref):
    o_ref[...] = jnp.dot(a_ref[...], w_ref[...], preferred_element_type=F32).astype(o_ref.dtype)


def _matmul(a, w, out_dtype):
    m, k = a.shape
    n = w.shape[1]
    tm, tn = min(m, 1024), min(n, 1024)
    return pl.pallas_call(
        _mm_kernel,
        grid=(m // tm, n // tn),
        in_specs=[pl.BlockSpec((tm, k), lambda i, j: (i, 0)),
                  pl.BlockSpec((k, tn), lambda i, j: (0, j))],
        out_specs=pl.BlockSpec((tm, tn), lambda i, j: (i, j)),
        out_shape=jax.ShapeDtypeStruct((m, n), out_dtype),
        compiler_params=_params("parallel", "parallel"),
        name="proj_matmul",
    )(a, w)


def _fox_gate_kernel(x_ref, wf_ref, bf_ref, c_ref, carry_ref):
    @pl.when(pl.program_id(1) == 0)
    def _():
        carry_ref[...] = jnp.zeros_like(carry_ref)

    z = jnp.dot(x_ref[...], wf_ref[...], precision=_HIGHEST, preferred_element_type=F32) + bf_ref[...]
    log_f = jnp.minimum(z, 0.0) - jnp.log1p(jnp.exp(-jnp.abs(z)))
    tc = z.shape[0]
    row = lax.broadcasted_iota(I32, (tc, tc), 0)
    col = lax.broadcasted_iota(I32, (tc, tc), 1)
    tri = (col <= row).astype(F32)
    c = jnp.dot(tri, log_f, precision=_HIGHEST, preferred_element_type=F32) + carry_ref[...]
    c_ref[...] = c
    carry_ref[...] = c[tc - 1:tc, :]


def _fox_gate_cumsum(x2, w_f, b_f, batch, seq):
    t, d = x2.shape
    h = w_f.shape[1]
    wf = jnp.zeros((d, _LANES), F32).at[:, :h].set(w_f)
    bf = jnp.zeros((1, _LANES), F32).at[0, :h].set(b_f)
    tc = min(seq, 512)
    ns = seq // tc
    return pl.pallas_call(
        _fox_gate_kernel,
        grid=(batch, ns),
        in_specs=[pl.BlockSpec((tc, d), lambda b, s: (b * ns + s, 0)),
                  pl.BlockSpec((d, _LANES), lambda b, s: (0, 0)),
                  pl.BlockSpec((1, _LANES), lambda b, s: (0, 0))],
        out_specs=pl.BlockSpec((tc, _LANES), lambda b, s: (b * ns + s, 0)),
        out_shape=jax.ShapeDtypeStruct((t, _LANES), F32),
        scratch_shapes=[pltpu.VMEM((1, _LANES), F32)],
        compiler_params=_params("parallel", "arbitrary"),
        name="fox_gate_cumsum",
    )(x2, wf, bf)


def _softmax_first(s, v):
    m = jnp.max(s, axis=1, keepdims=True)
    p = jnp.exp(s - m)
    l = jnp.sum(p, axis=1, keepdims=True)
    acc = jnp.dot(p.astype(BF16), v, preferred_element_type=F32)
    return m, l, acc


def _softmax_next(carry, s, v):
    m, l, acc = carry
    m_new = jnp.maximum(m, jnp.max(s, axis=1, keepdims=True))
    a = jnp.exp(m - m_new)
    p = jnp.exp(s - m_new)
    l = a * l + jnp.sum(p, axis=1, keepdims=True)
    acc = a * acc + jnp.dot(p.astype(BF16), v, preferred_element_type=F32)
    return m_new, l, acc


def _fox_attn_kernel(q_ref, k_ref, v_ref, cq_ref, ck_ref, o_ref, *, scale):
    h = pl.program_id(1)
    i = pl.program_id(2)
    tq = q_ref.shape[0]
    q = q_ref[...]
    lane = lax.broadcasted_iota(I32, cq_ref.shape, 1)
    cq = jnp.sum(jnp.where(lane == h, cq_ref[...], 0.0), axis=1, keepdims=True)

    def scores(j):
        start = pl.multiple_of(j * tq, tq)
        k = k_ref[pl.ds(start, tq), :]
        s = lax.dot_general(q, k, _NT, preferred_element_type=F32) * scale
        ck = ck_ref[0, pl.ds(h, 1), pl.ds(start, tq)]
        return s + cq - ck, v_ref[pl.ds(start, tq), :]

    s, v = scores(i)
    row = lax.broadcasted_iota(I32, (tq, tq), 0)
    col = lax.broadcasted_iota(I32, (tq, tq), 1)
    carry = _softmax_first(jnp.where(col <= row, s, _MASKED), v)

    def body(j, carry):
        s, v = scores(j)
        return _softmax_next(carry, s, v)

    _, l, acc = lax.fori_loop(0, i, body, carry)
    o_ref[...] = (acc / l).astype(o_ref.dtype)


def _fox_attention(qkv, c_pad, batch, seq, heads):
    t = qkv.shape[0]
    d = qkv.shape[1] // 3
    hd = d // heads
    tq = min(seq, 512)
    nq = seq // tq
    c_t = jnp.transpose(c_pad[:, :heads].reshape(batch, seq, heads), (0, 2, 1))
    return pl.pallas_call(
        functools.partial(_fox_attn_kernel, scale=hd ** -0.5),
        grid=(batch, heads, nq),
        in_specs=[pl.BlockSpec((tq, hd), lambda b, h, i: (b * nq + i, h)),
                  pl.BlockSpec((seq, hd), lambda b, h, i: (b, heads + h)),
                  pl.BlockSpec((seq, hd), lambda b, h, i: (b, 2 * heads + h)),
                  pl.BlockSpec((tq, _LANES), lambda b, h, i: (b * nq + i, 0)),
                  pl.BlockSpec((1, heads, seq), lambda b, h, i: (b, 0, 0))],
        out_specs=pl.BlockSpec((tq, hd), lambda b, h, i: (b * nq + i, h)),
        out_shape=jax.ShapeDtypeStruct((t, d), BF16),
        compiler_params=_params("parallel", "parallel", "parallel"),
        name="fox_attention",
    )(qkv, qkv, qkv, c_pad, c_t)


def _moba_attn_kernel(q_ref, k_ref, v_ref, o_ref, kmean_ref, *, scale, blk, topk):
    i = pl.program_id(2)
    nb = k_ref.shape[0] // blk

    @pl.when(i == 0)
    def _():
        kmean_ref[...] = jnp.zeros_like(kmean_ref)
        for n in range(nb):
            kb = k_ref[n * blk:(n + 1) * blk, :].astype(F32)
            kmean_ref[n:n + 1, :] = jnp.mean(kb, axis=0, keepdims=True)

    q = q_ref[...]
    tq = q.shape[0]
    gate = lax.dot_general(q.astype(F32), kmean_ref[...], _NT, precision=_HIGHEST,
                           preferred_element_type=F32)
    lane = lax.broadcasted_iota(I32, gate.shape, 1)
    lane_f = lane.astype(F32)
    gate = jnp.where(lane < i, gate, -jnp.inf)
    sel = jnp.zeros(gate.shape, F32)
    for r in range(topk):
        mx = jnp.max(gate, axis=1, keepdims=True)
        idx = jnp.min(jnp.where(gate == mx, lane_f, float(_LANES)), axis=1, keepdims=True)
        pick = lane_f == idx
        sel = jnp.where(jnp.logical_and(pick, r < i), 1.0, sel)
        gate = jnp.where(pick, -jnp.inf, gate)

    def scores(j):
        start = pl.multiple_of(j * blk, blk)
        k = k_ref[pl.ds(start, blk), :]
        s = lax.dot_general(q, k, _NT, preferred_element_type=F32) * scale
        return s, v_ref[pl.ds(start, blk), :]

    s, v = scores(i)
    row = lax.broadcasted_iota(I32, (tq, blk), 0)
    col = lax.broadcasted_iota(I32, (tq, blk), 1)
    carry = _softmax_first(jnp.where(col <= row, s, _MASKED), v)

    def body(j, carry):
        s, v = scores(j)
        chosen = jnp.sum(jnp.where(lane == j, sel, 0.0), axis=1, keepdims=True)
        return _softmax_next(carry, jnp.where(chosen > 0.0, s, _MASKED), v)

    _, l, acc = lax.fori_loop(0, i, body, carry)
    o_ref[...] = (acc / l).astype(o_ref.dtype)


def _moba_attention(qkv, batch, seq, heads):
    t = qkv.shape[0]
    d = qkv.shape[1] // 3
    hd = d // heads
    blk = MOBA_BLOCK
    assert seq % blk == 0 and seq // blk <= _LANES
    nq = seq // blk
    return pl.pallas_call(
        functools.partial(_moba_attn_kernel, scale=hd ** -0.5, blk=blk, topk=min(MOBA_TOPK, nq)),
        grid=(batch, heads, nq),
        in_specs=[pl.BlockSpec((blk, hd), lambda b, h, i: (b * nq + i, h)),
                  pl.BlockSpec((seq, hd), lambda b, h, i: (b, heads + h)),
                  pl.BlockSpec((seq, hd), lambda b, h, i: (b, 2 * heads + h))],
        out_specs=pl.BlockSpec((blk, hd), lambda b, h, i: (b * nq + i, h)),
        out_shape=jax.ShapeDtypeStruct((t, d), BF16),
        scratch_shapes=[pltpu.VMEM((_LANES, hd), F32)],
        compiler_params=_params("parallel", "parallel", "arbitrary"),
        name="moba_attention",
    )(qkv, qkv, qkv)


def _retention_kernel(q_ref, k_ref, v_ref, g_ref, cos_ref, sin_ref, dm_ref, xi_ref, zeta_ref, gc_ref,
                      gn_ref, o_ref, state_ref, *, k_scale):
    @pl.when(pl.program_id(2) == 0)
    def _():
        state_ref[...] = jnp.zeros_like(state_ref)

    cos, sin = cos_ref[...], sin_ref[...]
    half = cos.shape[1]

    def rotate(t):
        t = t.astype(F32)
        t1, t2 = t[:, :half], t[:, half:]
        return jnp.concatenate([t1 * cos - t2 * sin, t1 * sin + t2 * cos], axis=1)

    q = rotate(q_ref[...]).astype(BF16)
    k_rot = rotate(k_ref[...]) * k_scale
    k = k_rot.astype(BF16)
    k_t = k_rot.T.astype(BF16)
    v = v_ref[...]
    s = lax.dot_general(q, k, _NT, preferred_element_type=F32) * dm_ref[0]
    inner = jnp.dot(s.astype(BF16), v, preferred_element_type=F32)
    state = state_ref[...]
    cross = jnp.dot(q, state.astype(BF16), preferred_element_type=F32) * xi_ref[0]
    o = inner + cross
    vz = (v.astype(F32) * zeta_ref[0]).astype(BF16)
    state_ref[...] = gc_ref[0][:, :1] * state + jnp.dot(k_t, vz, preferred_element_type=F32)

    mu = jnp.mean(o, axis=-1, keepdims=True)
    d = o - mu
    var = jnp.mean(d * d, axis=-1, keepdims=True)
    o = d * lax.rsqrt(var + RET_GN_EPS) * gn_ref[...]
    g = g_ref[...].astype(F32)
    o_ref[...] = (g / (1.0 + jnp.exp(-g)) * o).astype(o_ref.dtype)


def _retention(proj, gn_g, batch, seq, heads):
    t, width = proj.shape
    dk = width // (6 * heads)
    dv = 2 * dk
    chunk = min(seq, 256)
    nc = seq // chunk
    pos = jnp.arange(seq, dtype=F32)
    inv_freq = jnp.exp(-jnp.log(RET_ROPE_BASE) * jnp.arange(0, dk, 2, dtype=F32) / dk)
    ang = pos[:, None] * inv_freq[None, :]
    cos, sin = jnp.cos(ang), jnp.sin(ang)
    log_gamma = jnp.log(1.0 - jnp.exp2(-5.0 - jnp.arange(heads, dtype=F32)))
    n = jnp.arange(chunk, dtype=F32)
    diff = n[:, None] - n[None, :]
    d_mask = jnp.where(diff[None] >= 0, jnp.exp(diff[None] * log_gamma[:, None, None]), 0.0)
    xi = jnp.exp((n[None, :] + 1.0) * log_gamma[:, None])[:, :, None]
    zeta = jnp.exp((chunk - 1.0 - n[None, :]) * log_gamma[:, None])[:, :, None]
    g_chunk = jnp.broadcast_to(jnp.exp(chunk * log_gamma)[:, None, None], (heads, 1, _LANES))
    return pl.pallas_call(
        functools.partial(_retention_kernel, k_scale=dk ** -0.5),
        grid=(batch, heads, nc),
        in_specs=[pl.BlockSpec((chunk, dk), lambda b, h, c: (b * nc + c, h)),
                  pl.BlockSpec((chunk, dk), lambda b, h, c: (b * nc + c, heads + h)),
                  pl.BlockSpec((chunk, dv), lambda b, h, c: (b * nc + c, heads + h)),
                  pl.BlockSpec((chunk, dv), lambda b, h, c: (b * nc + c, 2 * heads + h)),
                  pl.BlockSpec((chunk, dk // 2), lambda b, h, c: (c, 0)),
                  pl.BlockSpec((chunk, dk // 2), lambda b, h, c: (c, 0)),
                  pl.BlockSpec((1, chunk, chunk), lambda b, h, c: (h, 0, 0)),
                  pl.BlockSpec((1, chunk, 1), lambda b, h, c: (h, 0, 0)),
                  pl.BlockSpec((1, chunk, 1), lambda b, h, c: (h, 0, 0)),
                  pl.BlockSpec((1, 1, _LANES), lambda b, h, c: (h, 0, 0)),
                  pl.BlockSpec((1, dv), lambda b, h, c: (0, h))],
        out_specs=pl.BlockSpec((chunk, dv), lambda b, h, c: (b * nc + c, h)),
        out_shape=jax.ShapeDtypeStruct((t, heads * dv), BF16),
        scratch_shapes=[pltpu.VMEM((dk, dv), F32)],
        compiler_params=_params("parallel", "parallel", "arbitrary"),
        name="retention",
    )(proj, proj, proj, proj, cos, sin, d_mask, xi, zeta, g_chunk, gn_g.reshape(1, -1))


def _post_mixer_kernel(o_ref, w_ref, x_ref, g_ref, b_ref, rw_ref, rb_ref,
                       x1_ref, eidx_ref, gate_ref, rank_ref, cnt_ref, run_ref, *, n_exp, topk):
    @pl.when(pl.program_id(0) == 0)
    def _():
        run_ref[...] = jnp.zeros_like(run_ref)

    h = jnp.dot(o_ref[...], w_ref[...], preferred_element_type=F32)
    x1 = _layer_norm(DEEPNORM_ALPHA * x_ref[...] + h, g_ref[...], b_ref[...])
    x1_ref[...] = x1

    logits = jnp.dot(x1, rw_ref[...], precision=_HIGHEST, preferred_element_type=F32) + rb_ref[...]
    tm = logits.shape[0]
    lane = lax.broadcasted_iota(I32, logits.shape, 1).astype(F32)
    logits = jnp.where(lane < n_exp, logits, -jnp.inf)
    vals, idxs = [], []
    for _ in range(topk):
        mx = jnp.max(logits, axis=1, keepdims=True)
        idx = jnp.min(jnp.where(logits == mx, lane, float(_LANES)), axis=1, keepdims=True)
        vals.append(mx)
        idxs.append(idx)
        logits = jnp.where(lane == idx, -jnp.inf, logits)
    exps = [jnp.exp(v - vals[0]) for v in vals]
    den = exps[0]
    for e in exps[1:]:
        den = den + e

    picked = jnp.zeros(logits.shape, F32)
    for idx in idxs:
        picked = jnp.where(lane == idx, 1.0, picked)
    row = lax.broadcasted_iota(I32, (tm, tm), 0)
    col = lax.broadcasted_iota(I32, (tm, tm), 1)
    before = (col < row).astype(BF16)
    prefix = jnp.dot(before, picked.astype(BF16), preferred_element_type=F32) + run_ref[...]

    eidx = jnp.zeros(logits.shape, F32)
    gates = jnp.zeros(logits.shape, F32)
    ranks = jnp.zeros(logits.shape, F32)
    for r in range(topk):
        rank_r = jnp.sum(jnp.where(lane == idxs[r], prefix, 0.0), axis=1, keepdims=True)
        eidx = jnp.where(lane == r, idxs[r], eidx)
        gates = jnp.where(lane == r, exps[r] / den, gates)
        ranks = jnp.where(lane == r, rank_r, ranks)
    eidx_ref[...] = eidx.astype(I32)
    gate_ref[...] = gates
    rank_ref[...] = ranks.astype(I32)
    run = run_ref[...] + jnp.sum(picked, axis=0, keepdims=True)
    run_ref[...] = run
    cnt_ref[...] = jnp.broadcast_to(run, cnt_ref.shape).astype(I32)


def _post_mixer(o, w_out, x2, ln_g, ln_b, router_w, router_b):
    t, d = x2.shape
    kdim = o.shape[1]
    n_exp = router_w.shape[1]
    tm = min(t, 512)
    rw = jnp.zeros((d, _LANES), F32).at[:, :n_exp].set(router_w)
    rb = jnp.zeros((1, _LANES), F32).at[0, :n_exp].set(router_b)
    row = lambda i: (i, 0)
    fixed = lambda i: (0, 0)
    return pl.pallas_call(
        functools.partial(_post_mixer_kernel, n_exp=n_exp, topk=TOP_K),
        grid=(t // tm,),
        in_specs=[pl.BlockSpec((tm, kdim), row), pl.BlockSpec((kdim, d), fixed),
                  pl.BlockSpec((tm, d), row), pl.BlockSpec((1, d), fixed), pl.BlockSpec((1, d), fixed),
                  pl.BlockSpec((d, _LANES), fixed), pl.BlockSpec((1, _LANES), fixed)],
        out_specs=[pl.BlockSpec((tm, d), row), pl.BlockSpec((tm, _LANES), row),
                   pl.BlockSpec((tm, _LANES), row), pl.BlockSpec((tm, _LANES), row),
                   pl.BlockSpec((8, _LANES), fixed)],
        out_shape=[jax.ShapeDtypeStruct((t, d), F32), jax.ShapeDtypeStruct((t, _LANES), I32),
                   jax.ShapeDtypeStruct((t, _LANES), F32), jax.ShapeDtypeStruct((t, _LANES), I32),
                   jax.ShapeDtypeStruct((8, _LANES), I32)],
        scratch_shapes=[pltpu.VMEM((1, _LANES), F32)],
        compiler_params=_params("arbitrary"),
        name="post_mixer",
    )(o, w_out, x2, ln_g.reshape(1, d), ln_b.reshape(1, d), rw, rb)


def _dispatch_kernel(dest_ref, x_ref, xs_in_ref, xs_ref, sem, *, rows):
    del xs_in_ref
    base = pl.program_id(0) * rows
    n_copies = rows * TOP_K

    def row_copy(n):
        a = base * TOP_K + n
        return pltpu.make_async_copy(x_ref.at[pl.ds(a // TOP_K, 1)], xs_ref.at[pl.ds(dest_ref[a], 1)], sem)

    def start(n, c):
        row_copy(n).start()
        return c

    def wait(n, c):
        row_copy(n).wait()
        return c

    lax.fori_loop(0, n_copies, start, 0)
    lax.fori_loop(0, n_copies, wait, 0)


def _dispatch(dest_flat, x1, n_rows):
    t, d = x1.shape
    rows = min(t, 256)
    xs0 = jnp.zeros((n_rows, d), x1.dtype)
    return pl.pallas_call(
        functools.partial(_dispatch_kernel, rows=rows),
        grid_spec=pltpu.PrefetchScalarGridSpec(
            num_scalar_prefetch=1,
            grid=(t // rows,),
            in_specs=[pl.BlockSpec(memory_space=pl.ANY), pl.BlockSpec(memory_space=pl.ANY)],
            out_specs=pl.BlockSpec(memory_space=pl.ANY),
            scratch_shapes=[pltpu.SemaphoreType.DMA(())]),
        out_shape=jax.ShapeDtypeStruct((n_rows, d), x1.dtype),
        input_output_aliases={2: 0},
        compiler_params=_params("arbitrary"),
        name="moe_dispatch",
    )(dest_flat, x1, xs0)


def _ffn_kernel(blk_e_ref, n_used_ref, xs_ref, w1_ref, b1_ref, w2_ref, b2_ref, ys_ref, w1b_ref, w2b_ref):
    i = pl.program_id(0)
    e = blk_e_ref[i]
    e_prev = blk_e_ref[jnp.maximum(i - 1, 0)]

    @pl.when(jnp.logical_or(i == 0, e != e_prev))
    def _():
        w1b_ref[...] = w1_ref[0, 0].astype(BF16)
        w2b_ref[...] = w2_ref[0, 0].astype(BF16)

    @pl.when(i < n_used_ref[0])
    def _():
        f = w2b_ref.shape[0]
        h = jnp.dot(xs_ref[...].astype(BF16), w1b_ref[...], preferred_element_type=F32) + b1_ref[0, 0]
        glu = jnp.minimum(h[:, :f], SWIGLU_LIMIT)
        lin = jnp.clip(h[:, f:], -SWIGLU_LIMIT, SWIGLU_LIMIT)
        act = glu / (1.0 + jnp.exp(-SWIGLU_ALPHA * glu)) * (lin + 1.0)
        ys_ref[...] = jnp.dot(act.astype(BF16), w2b_ref[...], preferred_element_type=F32) + b2_ref[0, 0]

    @pl.when(i >= n_used_ref[0])
    def _():
        ys_ref[...] = jnp.zeros_like(ys_ref)


def _expert_ffn(blk_e, n_used, xs, layer, w1, b1, w2, b2):
    n_rows, d = xs.shape
    depth, n_exp, _, f2 = w1.shape
    f = w2.shape[2]
    n_blocks = n_rows // MOE_BLOCK
    by_expert = lambda i, be, nu: (layer, be[i], 0, 0)
    return pl.pallas_call(
        _ffn_kernel,
        grid_spec=pltpu.PrefetchScalarGridSpec(
            num_scalar_prefetch=2,
            grid=(n_blocks,),
            in_specs=[pl.BlockSpec((MOE_BLOCK, d), lambda i, be, nu: (i, 0)),
                      pl.BlockSpec((1, 1, d, f2), by_expert), pl.BlockSpec((1, 1, 1, f2), by_expert),
                      pl.BlockSpec((1, 1, f, d), by_expert), pl.BlockSpec((1, 1, 1, d), by_expert)],
            out_specs=pl.BlockSpec((MOE_BLOCK, d), lambda i, be, nu: (i, 0)),
            scratch_shapes=[pltpu.VMEM((d, f2), BF16), pltpu.VMEM((f, d), BF16)]),
        out_shape=jax.ShapeDtypeStruct((n_rows, d), F32),
        compiler_params=_params("arbitrary"),
        name="moe_expert_ffn",
    )(blk_e, n_used, xs, w1, b1.reshape(depth, n_exp, 1, f2), w2, b2.reshape(depth, n_exp, 1, d))


def _combine_kernel(dest_ref, ys_ref, gate_ref, x1_ref, g_ref, b_ref, x2_ref, x2b_ref, buf_ref, sem, *, rows):
    base = pl.program_id(0) * rows
    n_copies = rows * TOP_K

    def row_copy(n):
        return pltpu.make_async_copy(ys_ref.at[pl.ds(dest_ref[base * TOP_K + n], 1)],
                                     buf_ref.at[n % TOP_K, pl.ds(n // TOP_K, 1)], sem)

    def start(n, c):
        row_copy(n).start()
        return c

    def wait(n, c):
        row_copy(n).wait()
        return c

    lax.fori_loop(0, n_copies, start, 0)
    lax.fori_loop(0, n_copies, wait, 0)
    y = gate_ref[:, 0:1] * buf_ref[0]
    for k in range(1, TOP_K):
        y = y + gate_ref[:, k:k + 1] * buf_ref[k]
    x2 = _layer_norm(DEEPNORM_ALPHA * x1_ref[...] + y, g_ref[...], b_ref[...])
    x2_ref[...] = x2
    x2b_ref[...] = x2.astype(BF16)


def _combine(dest_flat, ys, gates, x1, ln_g, ln_b):
    t, d = x1.shape
    rows = min(t, 256)
    row = lambda i, dest: (i, 0)
    fixed = lambda i, dest: (0, 0)
    return pl.pallas_call(
        functools.partial(_combine_kernel, rows=rows),
        grid_spec=pltpu.PrefetchScalarGridSpec(
            num_scalar_prefetch=1,
            grid=(t // rows,),
            in_specs=[pl.BlockSpec(memory_space=pl.ANY), pl.BlockSpec((rows, _LANES), row),
                      pl.BlockSpec((rows, d), row), pl.BlockSpec((1, d), fixed), pl.BlockSpec((1, d), fixed)],
            out_specs=[pl.BlockSpec((rows, d), row), pl.BlockSpec((rows, d), row)],
            scratch_shapes=[pltpu.VMEM((TOP_K, rows, d), F32), pltpu.SemaphoreType.DMA(())]),
        out_shape=[jax.ShapeDtypeStruct((t, d), F32), jax.ShapeDtypeStruct((t, d), BF16)],
        compiler_params=_params("arbitrary"),
        name="moe_combine",
    )(dest_flat, ys, gates, x1, ln_g.reshape(1, d), ln_b.reshape(1, d))


def _moe_layer(x1, eidx, gates, rank, counts, layer, w1, b1, w2, b2, ln_g, ln_b):
    t, d = x1.shape
    n_exp = w1.shape[1]
    n_blocks = -(-(t * TOP_K) // MOE_BLOCK) + n_exp
    cnt = counts[0, :n_exp]
    padded = ((cnt + MOE_BLOCK - 1) // MOE_BLOCK) * MOE_BLOCK
    pad_ends = jnp.cumsum(padded)
    pad_starts = pad_ends - padded
    dest = (pad_starts[eidx[:, :TOP_K]] + rank[:, :TOP_K]).astype(I32).reshape(-1)
    blk_start = jnp.arange(n_blocks, dtype=I32) * MOE_BLOCK
    blk_e = jnp.minimum(jnp.searchsorted(pad_ends, blk_start, side='right'), n_exp - 1).astype(I32)
    n_used = (pad_ends[-1:] // MOE_BLOCK).astype(I32)
    xs = _dispatch(dest, x1, n_blocks * MOE_BLOCK)
    ys = _expert_ffn(blk_e, n_used, xs, layer, w1, b1, w2, b2)
    return _combine(dest, ys, gates, x1, ln_g, ln_b)


def _fox_mixer(x2, xb, w_in, b_f, batch, seq):
    d = x2.shape[1]
    qkv = _matmul(xb, w_in[:, :3 * d].astype(BF16), BF16)
    c_pad = _fox_gate_cumsum(x2, w_in[:, 3 * d:], b_f, batch, seq)
    return _fox_attention(qkv, c_pad, batch, seq, FOX_HEADS)


def _moba_mixer(xb, w_in, batch, seq):
    return _moba_attention(_matmul(xb, w_in.astype(BF16), BF16), batch, seq, MOBA_HEADS)


def _retention_mixer(xb, w_in, gn_g, batch, seq):
    return _retention(_matmul(xb, w_in.astype(BF16), BF16), gn_g, batch, seq, RET_HEADS)


def kernel(x, fox_w_in, fox_b_f, fox_w_out, moba_w_in, moba_w_out, ret_w_in, ret_gn_g, ret_w_out, ln_g, ln_b,
           router_w, router_b, moe_w1, moe_b1, moe_w2, moe_b2):
    batch, seq, d = x.shape
    x2 = x.reshape(batch * seq, d)
    xb = x2.astype(BF16)
    for i in range(DEPTH):
        kind, j = i % 3, i // 3
        if kind == 0:
            o, w_out = _fox_mixer(x2, xb, fox_w_in[j], fox_b_f[j], batch, seq), fox_w_out[j]
        elif kind == 1:
            o, w_out = _moba_mixer(xb, moba_w_in[j], batch, seq), moba_w_out[j]
        else:
            o, w_out = _retention_mixer(xb, ret_w_in[j], ret_gn_g[j], batch, seq), ret_w_out[j]
        x1, eidx, gates, rank, counts = _post_mixer(o, w_out.astype(BF16), x2, ln_g[i, 0], ln_b[i, 0],
                                                    router_w[i], router_b[i])
        x2, xb = _moe_layer(x1, eidx, gates, rank, counts, i, moe_w1, moe_b1, moe_w2, moe_b2,
                            ln_g[i, 1], ln_b[i, 1])
    return x2.reshape(batch, seq, d)
```

```python
import functools

import jax
import jax.numpy as jnp
from jax import lax
from jax.experimental import pallas as pl
from jax.experimental.pallas import tpu as pltpu

F32, BF16, I32 = jnp.float32, jnp.bfloat16, jnp.int32
_HIGHEST = lax.Precision.HIGHEST
_NT = (((1,), (1,)), ((), ()))
_LANES = 128
_SUBLANES = 8
_MASKED = -1e30
_BIAS_OFF = -32768.0
_LOG2E = 1.4426950408889634
_VMEM_LIMIT = 56 * 1024 * 1024

DEPTH = 4
FOX_HEADS, MOBA_HEADS, RET_HEADS = 8, 8, 4
MOBA_BLOCK, MOBA_TOPK = 256, 3
RET_ROPE_BASE, RET_GN_EPS = 10000.0, 1e-6
N_EXPERTS, TOP_K, MOE_BLOCK = 32, 4, 256
SWIGLU_LIMIT, SWIGLU_ALPHA = 7.0, 1.702
LN_EPS = 1e-5
DEEPNORM_ALPHA = (2 * DEPTH) ** 0.25


def _params(*sem):
    return pltpu.CompilerParams(dimension_semantics=sem, vmem_limit_bytes=_VMEM_LIMIT)


def _layer_norm(y, g, b):
    mu = jnp.mean(y, axis=-1, keepdims=True)
    d = y - mu
    var = jnp.mean(d * d, axis=-1, keepdims=True)
    return d * lax.rsqrt(var + LN_EPS) * g + b


def _store_slabs(ref, val):
    n = val.shape[0]
    for s in range(_SUBLANES):
        ref[pl.ds(s, n, stride=_SUBLANES), :] = val[:, s * _LANES:(s + 1) * _LANES]


def _load_slabs(ref, n):
    return jnp.concatenate([ref[pl.ds(s, n, stride=_SUBLANES), :] for s in range(_SUBLANES)], axis=1)


def _mm_kernel(a_ref, w_ref, o_ref):
    o_ref[...] = jnp.dot(a_ref[...], w_ref[...], preferred_element_type=F32).astype(o_ref.dtype)


def _matmul(a, w, out_dtype):
    m, k = a.shape
    n = w.shape[1]
    tm, tn = min(m, 1024), min(n, 1024)
    return pl.pallas_call(
        _mm_kernel,
        grid=(m // tm, n // tn),
        in_specs=[pl.BlockSpec((tm, k), lambda i, j: (i, 0)),
                  pl.BlockSpec((k, tn), lambda i, j: (0, j))],
        out_specs=pl.BlockSpec((tm, tn), lambda i, j: (i, j)),
        out_shape=jax.ShapeDtypeStruct((m, n), out_dtype),
        compiler_params=_params("parallel", "parallel"),
        name="proj_matmul",
    )(a, w)


def _flash_first(s, v):
    m = jnp.max(s, axis=1, keepdims=True)
    p = jnp.exp2(s - m)
    l = jnp.sum(p, axis=1, keepdims=True)
    acc = jnp.dot(p.astype(BF16), v, preferred_element_type=F32)
    return m, l, acc


def _flash_next(carry, s, v):
    m, l, acc = carry
    m_new = jnp.maximum(m, jnp.max(s, axis=1, keepdims=True))
    a = jnp.exp2(m - m_new)
    p = jnp.exp2(s - m_new)
    l = a * l + jnp.sum(p, axis=1, keepdims=True)
    acc = a * acc + jnp.dot(p.astype(BF16), v, preferred_element_type=F32)
    return m_new, l, acc


def _causal_flash(q_aug, kaug_ref, v_ref, i, tq):
    def tile(j):
        start = pl.multiple_of(j * tq, tq)
        s = lax.dot_general(q_aug, kaug_ref[pl.ds(start, tq), :], _NT, preferred_element_type=F32)
        return s, v_ref[pl.ds(start, tq), :]

    s, v = tile(i)
    row = lax.broadcasted_iota(I32, (tq, tq), 0)
    col = lax.broadcasted_iota(I32, (tq, tq), 1)
    carry = _flash_first(jnp.where(col <= row, s, _MASKED), v)
    _, l, acc = lax.fori_loop(0, i, lambda j, c: _flash_next(c, *tile(j)), carry)
    return acc / l


def _lane_columns(n, cols):
    lane = lax.broadcasted_iota(I32, (n, _LANES), 1)
    out = jnp.zeros((n, _LANES), F32)
    for t, c in enumerate(cols):
        out = jnp.where(lane == t, c, out)
    return out.astype(BF16)


def _split3(c):
    hi = c.astype(BF16).astype(F32)
    r = c - hi
    mid = r.astype(BF16).astype(F32)
    lo = (r - mid).astype(BF16).astype(F32)
    return [hi, mid, lo]


def _fox_gate_kernel(x_ref, wf_ref, bf_ref, c_ref, carry_ref):
    @pl.when(pl.program_id(1) == 0)
    def _():
        carry_ref[...] = jnp.zeros_like(carry_ref)

    z = jnp.dot(x_ref[...], wf_ref[...], precision=_HIGHEST, preferred_element_type=F32) + bf_ref[...]
    log_f = jnp.minimum(z, 0.0) - jnp.log1p(jnp.exp(-jnp.abs(z)))
    tc = z.shape[0]
    row = lax.broadcasted_iota(I32, (tc, tc), 0)
    col = lax.broadcasted_iota(I32, (tc, tc), 1)
    tri = (col <= row).astype(F32)
    c = jnp.dot(tri, log_f, precision=_HIGHEST, preferred_element_type=F32) + carry_ref[...]
    c_ref[...] = c
    carry_ref[...] = c[tc - 1:tc, :]


def _fox_gate_cumsum(x2, w_f, b_f, batch, seq):
    t, d = x2.shape
    h = w_f.shape[1]
    wf = jnp.zeros((d, _LANES), F32).at[:, :h].set(w_f)
    bf = jnp.zeros((1, _LANES), F32).at[0, :h].set(b_f)
    tc = min(seq, 512)
    ns = seq // tc
    return pl.pallas_call(
        _fox_gate_kernel,
        grid=(batch, ns),
        in_specs=[pl.BlockSpec((tc, d), lambda b, s: (b * ns + s, 0)),
                  pl.BlockSpec((d, _LANES), lambda b, s: (0, 0)),
                  pl.BlockSpec((1, _LANES), lambda b, s: (0, 0))],
        out_specs=pl.BlockSpec((tc, _LANES), lambda b, s: (b * ns + s, 0)),
        out_shape=jax.ShapeDtypeStruct((t, _LANES), F32),
        scratch_shapes=[pltpu.VMEM((1, _LANES), F32)],
        compiler_params=_params("parallel", "arbitrary"),
        name="fox_gate_cumsum",
    )(x2, wf, bf)


def _fox_attn_kernel(q_ref, k_ref, v_ref, cq_ref, ck_ref, o_ref, kaug_ref, *, scale):
    h = pl.program_id(1)
    i = pl.program_id(2)
    tq, hd = q_ref.shape
    seq = k_ref.shape[0]

    def head_column(c_ref, rows):
        lane = lax.broadcasted_iota(I32, (rows, _LANES), 1)
        return jnp.sum(jnp.where(lane == h, c_ref[...], 0.0), axis=1, keepdims=True) * _LOG2E

    @pl.when(i == 0)
    def _():
        kaug_ref[:, :hd] = k_ref[...]
        kaug_ref[:, hd:] = _lane_columns(seq, [1.0, 1.0, 1.0] + [-c for c in _split3(head_column(ck_ref, seq))])

    q = (q_ref[...].astype(F32) * (scale * _LOG2E)).astype(BF16)
    q_aug = jnp.concatenate([q, _lane_columns(tq, _split3(head_column(cq_ref, tq)) + [1.0, 1.0, 1.0])], axis=1)
    o_ref[...] = _causal_flash(q_aug, kaug_ref, v_ref, i, tq).astype(o_ref.dtype)


def _fox_attention(qkv, c_pad, batch, seq, heads):
    t = qkv.shape[0]
    d = qkv.shape[1] // 3
    hd = d // heads
    tq = min(seq, 512)
    nq = seq // tq
    return pl.pallas_call(
        functools.partial(_fox_attn_kernel, scale=hd ** -0.5),
        grid=(batch, heads, nq),
        in_specs=[pl.BlockSpec((tq, hd), lambda b, h, i: (b * nq + i, h)),
                  pl.BlockSpec((seq, hd), lambda b, h, i: (b, heads + h)),
                  pl.BlockSpec((seq, hd), lambda b, h, i: (b, 2 * heads + h)),
                  pl.BlockSpec((tq, _LANES), lambda b, h, i: (b * nq + i, 0)),
                  pl.BlockSpec((seq, _LANES), lambda b, h, i: (b, 0))],
        out_specs=pl.BlockSpec((tq, hd), lambda b, h, i: (b * nq + i, h)),
        out_shape=jax.ShapeDtypeStruct((t, d), BF16),
        scratch_shapes=[pltpu.VMEM((seq, hd + _LANES), BF16)],
        compiler_params=_params("parallel", "parallel", "arbitrary"),
        name="fox_attention",
    )(qkv, qkv, qkv, c_pad, c_pad)


def _moba_attn_kernel(q_ref, k_ref, v_ref, o_ref, kaug_ref, kmean_ref, *, scale, blk, topk):
    i = pl.program_id(2)
    tq, hd = q_ref.shape
    seq = k_ref.shape[0]
    nb = seq // blk

    @pl.when(i == 0)
    def _():
        kmean_ref[...] = jnp.zeros_like(kmean_ref)
        for n in range(nb):
            kb = k_ref[n * blk:(n + 1) * blk, :].astype(F32)
            kmean_ref[n:n + 1, :] = jnp.mean(kb, axis=0, keepdims=True)
        kaug_ref[:, :hd] = k_ref[...]
        key_block = lax.broadcasted_iota(I32, (seq, _LANES), 0) // blk
        lane = lax.broadcasted_iota(I32, (seq, _LANES), 1)
        kaug_ref[:, hd:] = jnp.where(key_block == lane, 1.0, 0.0).astype(BF16)

    q = q_ref[...]
    gate = lax.dot_general(q.astype(F32), kmean_ref[...], _NT, precision=_HIGHEST,
                           preferred_element_type=F32)
    lane = lax.broadcasted_iota(I32, gate.shape, 1)
    lane_f = lane.astype(F32)
    own = (i * tq + lax.broadcasted_iota(I32, gate.shape, 0)) // blk
    gate = jnp.where(lane < own, gate, -jnp.inf)
    allowed = lane == own
    for r in range(topk):
        mx = jnp.max(gate, axis=1, keepdims=True)
        idx = jnp.min(jnp.where(gate == mx, lane_f, float(_LANES)), axis=1, keepdims=True)
        pick = lane_f == idx
        allowed = jnp.logical_or(allowed, jnp.logical_and(pick, r < own))
        gate = jnp.where(pick, -jnp.inf, gate)
    bias = jnp.where(allowed, 0.0, _BIAS_OFF).astype(BF16)
    q_aug = jnp.concatenate([(q.astype(F32) * (scale * _LOG2E)).astype(BF16), bias], axis=1)
    o_ref[...] = _causal_flash(q_aug, kaug_ref, v_ref, i, tq).astype(o_ref.dtype)


def _moba_attention(qkv, batch, seq, heads):
    t = qkv.shape[0]
    d = qkv.shape[1] // 3
    hd = d // heads
    blk = MOBA_BLOCK
    assert seq % blk == 0 and seq // blk <= _LANES
    tq = 2 * blk if seq % (2 * blk) == 0 else blk
    nq = seq // tq
    return pl.pallas_call(
        functools.partial(_moba_attn_kernel, scale=hd ** -0.5, blk=blk, topk=min(MOBA_TOPK, seq // blk)),
        grid=(batch, heads, nq),
        in_specs=[pl.BlockSpec((tq, hd), lambda b, h, i: (b * nq + i, h)),
                  pl.BlockSpec((seq, hd), lambda b, h, i: (b, heads + h)),
                  pl.BlockSpec((seq, hd), lambda b, h, i: (b, 2 * heads + h))],
        out_specs=pl.BlockSpec((tq, hd), lambda b, h, i: (b * nq + i, h)),
        out_shape=jax.ShapeDtypeStruct((t, d), BF16),
        scratch_shapes=[pltpu.VMEM((seq, hd + _LANES), BF16), pltpu.VMEM((_LANES, hd), F32)],
        compiler_params=_params("parallel", "parallel", "arbitrary"),
        name="moba_attention",
    )(qkv, qkv, qkv)


def _retention_kernel(q_ref, k_ref, v_ref, g_ref, cos_ref, sin_ref, dm_ref, xi_ref, zeta_ref, gc_ref,
                      gn_ref, o_ref, state_ref, *, k_scale):
    @pl.when(pl.program_id(2) == 0)
    def _():
        state_ref[...] = jnp.zeros_like(state_ref)

    cos, sin = cos_ref[...], sin_ref[...]
    half = cos.shape[1]

    def rotate(t):
        t = t.astype(F32)
        t1, t2 = t[:, :half], t[:, half:]
        return jnp.concatenate([t1 * cos - t2 * sin, t1 * sin + t2 * cos], axis=1)

    q = rotate(q_ref[...]).astype(BF16)
    k_rot = rotate(k_ref[...]) * k_scale
    k = k_rot.astype(BF16)
    k_t = k_rot.T.astype(BF16)
    v = v_ref[...]
    s = lax.dot_general(q, k, _NT, preferred_element_type=F32) * dm_ref[0]
    inner = jnp.dot(s.astype(BF16), v, preferred_element_type=F32)
    state = state_ref[...]
    cross = jnp.dot(q, state.astype(BF16), preferred_element_type=F32) * xi_ref[0]
    o = inner + cross
    vz = (v.astype(F32) * zeta_ref[0]).astype(BF16)
    state_ref[...] = gc_ref[0][:, :1] * state + jnp.dot(k_t, vz, preferred_element_type=F32)

    mu = jnp.mean(o, axis=-1, keepdims=True)
    d = o - mu
    var = jnp.mean(d * d, axis=-1, keepdims=True)
    o = d * lax.rsqrt(var + RET_GN_EPS) * gn_ref[...]
    g = g_ref[...].astype(F32)
    o_ref[...] = (g / (1.0 + jnp.exp(-g)) * o).astype(o_ref.dtype)


def _retention(proj, gn_g, batch, seq, heads):
    t, width = proj.shape
    dk = width // (6 * heads)
    dv = 2 * dk
    chunk = min(seq, 256)
    nc = seq // chunk
    pos = jnp.arange(seq, dtype=F32)
    inv_freq = jnp.exp(-jnp.log(RET_ROPE_BASE) * jnp.arange(0, dk, 2, dtype=F32) / dk)
    ang = pos[:, None] * inv_freq[None, :]
    cos, sin = jnp.cos(ang), jnp.sin(ang)
    log_gamma = jnp.log(1.0 - jnp.exp2(-5.0 - jnp.arange(heads, dtype=F32)))
    n = jnp.arange(chunk, dtype=F32)
    diff = n[:, None] - n[None, :]
    d_mask = jnp.where(diff[None] >= 0, jnp.exp(diff[None] * log_gamma[:, None, None]), 0.0)
    xi = jnp.exp((n[None, :] + 1.0) * log_gamma[:, None])[:, :, None]
    zeta = jnp.exp((chunk - 1.0 - n[None, :]) * log_gamma[:, None])[:, :, None]
    g_chunk = jnp.broadcast_to(jnp.exp(chunk * log_gamma)[:, None, None], (heads, 1, _LANES))
    return pl.pallas_call(
        functools.partial(_retention_kernel, k_scale=dk ** -0.5),
        grid=(batch, heads, nc),
        in_specs=[pl.BlockSpec((chunk, dk), lambda b, h, c: (b * nc + c, h)),
                  pl.BlockSpec((chunk, dk), lambda b, h, c: (b * nc + c, heads + h)),
                  pl.BlockSpec((chunk, dv), lambda b, h, c: (b * nc + c, heads + h)),
                  pl.BlockSpec((chunk, dv), lambda b, h, c: (b * nc + c, 2 * heads + h)),
                  pl.BlockSpec((chunk, dk // 2), lambda b, h, c: (c, 0)),
                  pl.BlockSpec((chunk, dk // 2), lambda b, h, c: (c, 0)),
                  pl.BlockSpec((1, chunk, chunk), lambda b, h, c: (h, 0, 0)),
                  pl.BlockSpec((1, chunk, 1), lambda b, h, c: (h, 0, 0)),
                  pl.BlockSpec((1, chunk, 1), lambda b, h, c: (h, 0, 0)),
                  pl.BlockSpec((1, 1, _LANES), lambda b, h, c: (h, 0, 0)),
                  pl.BlockSpec((1, dv), lambda b, h, c: (0, h))],
        out_specs=pl.BlockSpec((chunk, dv), lambda b, h, c: (b * nc + c, h)),
        out_shape=jax.ShapeDtypeStruct((t, heads * dv), BF16),
        scratch_shapes=[pltpu.VMEM((dk, dv), F32)],
        compiler_params=_params("parallel", "parallel", "arbitrary"),
        name="retention",
    )(proj, proj, proj, proj, cos, sin, d_mask, xi, zeta, g_chunk, gn_g.reshape(1, -1))


def _post_mixer_kernel(o_ref, w_ref, x_ref, g_ref, b_ref, rw_ref, rb_ref,
                       x1s_ref, eidx_ref, gate_ref, rank_ref, cnt_ref, run_ref, *, n_exp, topk):
    @pl.when(pl.program_id(0) == 0)
    def _():
        run_ref[...] = jnp.zeros_like(run_ref)

    h = jnp.dot(o_ref[...], w_ref[...], preferred_element_type=F32)
    x1 = _layer_norm(DEEPNORM_ALPHA * x_ref[...] + h, g_ref[...], b_ref[...])
    _store_slabs(x1s_ref, x1)

    logits = jnp.dot(x1, rw_ref[...], precision=_HIGHEST, preferred_element_type=F32) + rb_ref[...]
    tm = logits.shape[0]
    lane = lax.broadcasted_iota(I32, logits.shape, 1).astype(F32)
    logits = jnp.where(lane < n_exp, logits, -jnp.inf)
    vals, idxs = [], []
    for _ in range(topk):
        mx = jnp.max(logits, axis=1, keepdims=True)
        idx = jnp.min(jnp.where(logits == mx, lane, float(_LANES)), axis=1, keepdims=True)
        vals.append(mx)
        idxs.append(idx)
        logits = jnp.where(lane == idx, -jnp.inf, logits)
    exps = [jnp.exp(v - vals[0]) for v in vals]
    den = exps[0]
    for e in exps[1:]:
        den = den + e

    picked = jnp.zeros(logits.shape, F32)
    for idx in idxs:
        picked = jnp.where(lane == idx, 1.0, picked)
    row = lax.broadcasted_iota(I32, (tm, tm), 0)
    col = lax.broadcasted_iota(I32, (tm, tm), 1)
    before = (col < row).astype(BF16)
    prefix = jnp.dot(before, picked.astype(BF16), preferred_element_type=F32) + run_ref[...]

    eidx = jnp.zeros(logits.shape, F32)
    gates = jnp.zeros(logits.shape, F32)
    ranks = jnp.zeros(logits.shape, F32)
    for r in range(topk):
        rank_r = jnp.sum(jnp.where(lane == idxs[r], prefix, 0.0), axis=1, keepdims=True)
        eidx = jnp.where(lane == r, idxs[r], eidx)
        gates = jnp.where(lane == r, exps[r] / den, gates)
        ranks = jnp.where(lane == r, rank_r, ranks)
    eidx_ref[...] = eidx.astype(I32)
    gate_ref[...] = gates
    rank_ref[...] = ranks.astype(I32)
    run = run_ref[...] + jnp.sum(picked, axis=0, keepdims=True)
    run_ref[...] = run
    cnt_ref[...] = jnp.broadcast_to(run, cnt_ref.shape).astype(I32)


def _post_mixer(o, w_out, x2, ln_g, ln_b, router_w, router_b):
    t, d = x2.shape
    assert d == _SUBLANES * _LANES
    kdim = o.shape[1]
    n_exp = router_w.shape[1]
    tm = min(t, 512)
    rw = jnp.zeros((d, _LANES), F32).at[:, :n_exp].set(router_w)
    rb = jnp.zeros((1, _LANES), F32).at[0, :n_exp].set(router_b)
    row = lambda i: (i, 0)
    fixed = lambda i: (0, 0)
    return pl.pallas_call(
        functools.partial(_post_mixer_kernel, n_exp=n_exp, topk=TOP_K),
        grid=(t // tm,),
        in_specs=[pl.BlockSpec((tm, kdim), row), pl.BlockSpec((kdim, d), fixed),
                  pl.BlockSpec((tm, d), row), pl.BlockSpec((1, d), fixed), pl.BlockSpec((1, d), fixed),
                  pl.BlockSpec((d, _LANES), fixed), pl.BlockSpec((1, _LANES), fixed)],
        out_specs=[pl.BlockSpec((tm * _SUBLANES, _LANES), row), pl.BlockSpec((tm, _LANES), row),
                   pl.BlockSpec((tm, _LANES), row), pl.BlockSpec((tm, _LANES), row),
                   pl.BlockSpec((8, _LANES), fixed)],
        out_shape=[jax.ShapeDtypeStruct((t * _SUBLANES, _LANES), F32), jax.ShapeDtypeStruct((t, _LANES), I32),
                   jax.ShapeDtypeStruct((t, _LANES), F32), jax.ShapeDtypeStruct((t, _LANES), I32),
                   jax.ShapeDtypeStruct((8, _LANES), I32)],
        scratch_shapes=[pltpu.VMEM((1, _LANES), F32)],
        compiler_params=_params("arbitrary"),
        name="post_mixer",
    )(o, w_out, x2, ln_g.reshape(1, d), ln_b.reshape(1, d), rw, rb)


def _slab(ref, r):
    return ref.at[pl.ds(pl.multiple_of(r * _SUBLANES, _SUBLANES), _SUBLANES)]


def _dispatch_kernel(dest_ref, x_ref, xs_in_ref, xs_ref, sem, *, rows):
    del xs_in_ref
    base = pl.program_id(0) * rows * TOP_K
    n_copies = rows * TOP_K

    def row_copy(n):
        return pltpu.make_async_copy(_slab(x_ref, n // TOP_K), _slab(xs_ref, dest_ref[base + n]), sem)

    def start(n, c):
        row_copy(n).start()
        return c

    def wait(n, c):
        row_copy(n).wait()
        return c

    lax.fori_loop(0, n_copies, start, 0)
    lax.fori_loop(0, n_copies, wait, 0)


def _dispatch(dest_flat, x1s, n_rows):
    t = x1s.shape[0] // _SUBLANES
    rows = min(t, 256)
    xs0 = jnp.zeros((n_rows * _SUBLANES, _LANES), x1s.dtype)
    return pl.pallas_call(
        functools.partial(_dispatch_kernel, rows=rows),
        grid_spec=pltpu.PrefetchScalarGridSpec(
            num_scalar_prefetch=1,
            grid=(t // rows,),
            in_specs=[pl.BlockSpec((rows * _SUBLANES, _LANES), lambda i, dest: (i, 0)),
                      pl.BlockSpec(memory_space=pl.ANY)],
            out_specs=pl.BlockSpec(memory_space=pl.ANY),
            scratch_shapes=[pltpu.SemaphoreType.DMA(())]),
        out_shape=jax.ShapeDtypeStruct(xs0.shape, xs0.dtype),
        input_output_aliases={2: 0},
        compiler_params=_params("arbitrary"),
        name="moe_dispatch",
    )(dest_flat, x1s, xs0)


def _ffn_kernel(blk_e_ref, n_used_ref, xs_ref, w1_ref, b1_ref, w2_ref, b2_ref, ys_ref, w1b_ref, w2b_ref):
    i = pl.program_id(0)
    e = blk_e_ref[i]
    e_prev = blk_e_ref[jnp.maximum(i - 1, 0)]

    @pl.when(jnp.logical_or(i == 0, e != e_prev))
    def _():
        w1b_ref[...] = w1_ref[0, 0].astype(BF16)
        w2b_ref[...] = w2_ref[0, 0].astype(BF16)

    @pl.when(i < n_used_ref[0])
    def _():
        f = w2b_ref.shape[0]
        x = _load_slabs(xs_ref, MOE_BLOCK).astype(BF16)
        h = jnp.dot(x, w1b_ref[...], preferred_element_type=F32) + b1_ref[0, 0]
        glu = jnp.minimum(h[:, :f], SWIGLU_LIMIT)
        lin = jnp.clip(h[:, f:], -SWIGLU_LIMIT, SWIGLU_LIMIT)
        act = glu / (1.0 + jnp.exp(-SWIGLU_ALPHA * glu)) * (lin + 1.0)
        _store_slabs(ys_ref, jnp.dot(act.astype(BF16), w2b_ref[...], preferred_element_type=F32) + b2_ref[0, 0])

    @pl.when(i >= n_used_ref[0])
    def _():
        ys_ref[...] = jnp.zeros_like(ys_ref)


def _expert_ffn(blk_e, n_used, xs, layer, w1, b1, w2, b2):
    depth, n_exp, d, f2 = w1.shape
    f = w2.shape[2]
    n_blocks = xs.shape[0] // (MOE_BLOCK * _SUBLANES)
    by_expert = lambda i, be, nu: (layer, be[i], 0, 0)
    rows = pl.BlockSpec((MOE_BLOCK * _SUBLANES, _LANES), lambda i, be, nu: (i, 0))
    return pl.pallas_call(
        _ffn_kernel,
        grid_spec=pltpu.PrefetchScalarGridSpec(
            num_scalar_prefetch=2,
            grid=(n_blocks,),
            in_specs=[rows,
                      pl.BlockSpec((1, 1, d, f2), by_expert), pl.BlockSpec((1, 1, 1, f2), by_expert),
                      pl.BlockSpec((1, 1, f, d), by_expert), pl.BlockSpec((1, 1, 1, d), by_expert)],
            out_specs=rows,
            scratch_shapes=[pltpu.VMEM((d, f2), BF16), pltpu.VMEM((f, d), BF16)]),
        out_shape=jax.ShapeDtypeStruct(xs.shape, F32),
        compiler_params=_params("arbitrary"),
        name="moe_expert_ffn",
    )(blk_e, n_used, xs, w1, b1.reshape(depth, n_exp, 1, f2), w2, b2.reshape(depth, n_exp, 1, d))


def _combine_kernel(dest_ref, ys_ref, gate_ref, x1s_ref, g_ref, b_ref, x2_ref, x2b_ref, buf_ref, sem, *, rows):
    base = pl.program_id(0) * rows * TOP_K
    n_copies = rows * TOP_K

    def row_copy(n):
        return pltpu.make_async_copy(_slab(ys_ref, dest_ref[base + n]),
                                     _slab(buf_ref.at[n % TOP_K], n // TOP_K), sem)

    def start(n, c):
        row_copy(n).start()
        return c

    def wait(n, c):
        row_copy(n).wait()
        return c

    lax.fori_loop(0, n_copies, start, 0)
    lax.fori_loop(0, n_copies, wait, 0)
    y = gate_ref[:, 0:1] * _load_slabs(buf_ref.at[0], rows)
    for k in range(1, TOP_K):
        y = y + gate_ref[:, k:k + 1] * _load_slabs(buf_ref.at[k], rows)
    x2 = _layer_norm(DEEPNORM_ALPHA * _load_slabs(x1s_ref, rows) + y, g_ref[...], b_ref[...])
    x2_ref[...] = x2
    x2b_ref[...] = x2.astype(BF16)


def _combine(dest_flat, ys, gates, x1s, ln_g, ln_b):
    t = gates.shape[0]
    d = _SUBLANES * _LANES
    rows = min(t, 256)
    row = lambda i, dest: (i, 0)
    fixed = lambda i, dest: (0, 0)
    return pl.pallas_call(
        functools.partial(_combine_kernel, rows=rows),
        grid_spec=pltpu.PrefetchScalarGridSpec(
            num_scalar_prefetch=1,
            grid=(t // rows,),
            in_specs=[pl.BlockSpec(memory_space=pl.ANY), pl.BlockSpec((rows, _LANES), row),
                      pl.BlockSpec((rows * _SUBLANES, _LANES), row),
                      pl.BlockSpec((1, d), fixed), pl.BlockSpec((1, d), fixed)],
            out_specs=[pl.BlockSpec((rows, d), row), pl.BlockSpec((rows, d), row)],
            scratch_shapes=[pltpu.VMEM((TOP_K, rows * _SUBLANES, _LANES), F32), pltpu.SemaphoreType.DMA(())]),
        out_shape=[jax.ShapeDtypeStruct((t, d), F32), jax.ShapeDtypeStruct((t, d), BF16)],
        compiler_params=_params("arbitrary"),
        name="moe_combine",
    )(dest_flat, ys, gates, x1s, ln_g.reshape(1, d), ln_b.reshape(1, d))


def _moe_layer(x1s, eidx, gates, rank, counts, layer, w1, b1, w2, b2, ln_g, ln_b):
    t = gates.shape[0]
    n_exp = w1.shape[1]
    n_blocks = -(-(t * TOP_K) // MOE_BLOCK) + n_exp
    cnt = counts[0, :n_exp]
    padded = ((cnt + MOE_BLOCK - 1) // MOE_BLOCK) * MOE_BLOCK
    pad_ends = jnp.cumsum(padded)
    pad_starts = pad_ends - padded
    dest = (pad_starts[eidx[:, :TOP_K]] + rank[:, :TOP_K]).astype(I32).reshape(-1)
    blk_start = jnp.arange(n_blocks, dtype=I32) * MOE_BLOCK
    blk_e = jnp.minimum(jnp.searchsorted(pad_ends, blk_start, side='right'), n_exp - 1).astype(I32)
    n_used = (pad_ends[-1:] // MOE_BLOCK).astype(I32)
    xs = _dispatch(dest, x1s, n_blocks * MOE_BLOCK)
    ys = _expert_ffn(blk_e, n_used, xs, layer, w1, b1, w2, b2)
    return _combine(dest, ys, gates, x1s, ln_g, ln_b)


def _fox_mixer(x2, xb, w_in, b_f, batch, seq):
    d = x2.shape[1]
    qkv = _matmul(xb, w_in[:, :3 * d].astype(BF16), BF16)
    c_pad = _fox_gate_cumsum(x2, w_in[:, 3 * d:], b_f, batch, seq)
    return _fox_attention(qkv, c_pad, batch, seq, FOX_HEADS)


def _moba_mixer(xb, w_in, batch, seq):
    return _moba_attention(_matmul(xb, w_in.astype(BF16), BF16), batch, seq, MOBA_HEADS)


def _retention_mixer(xb, w_in, gn_g, batch, seq):
    return _retention(_matmul(xb, w_in.astype(BF16), BF16), gn_g, batch, seq, RET_HEADS)


def kernel(x, fox_w_in, fox_b_f, fox_w_out, moba_w_in, moba_w_out, ret_w_in, ret_gn_g, ret_w_out, ln_g, ln_b,
           router_w, router_b, moe_w1, moe_b1, moe_w2, moe_b2):
    batch, seq, d = x.shape
    x2 = x.reshape(batch * seq, d)
    xb = x2.astype(BF16)
    for i in range(DEPTH):
        kind, j = i % 3, i // 3
        if kind == 0:
            o, w_out = _fox_mixer(x2, xb, fox_w_in[j], fox_b_f[j], batch, seq), fox_w_out[j]
        elif kind == 1:
            o, w_out = _moba_mixer(xb, moba_w_in[j], batch, seq), moba_w_out[j]
        else:
            o, w_out = _retention_mixer(xb, ret_w_in[j], ret_gn_g[j], batch, seq), ret_w_out[j]
        x1s, eidx, gates, rank, counts = _post_mixer(o, w_out.astype(BF16), x2, ln_g[i, 0], ln_b[i, 0],
                                                     router_w[i], router_b[i])
        x2, xb = _moe_layer(x1s, eidx, gates, rank, counts, i, moe_w1, moe_b1, moe_w2, moe_b2,
                            ln_g[i, 1], ln_b[i, 1])
    return x2.reshape(batch, seq, d)
```

```python
import functools

import jax
import jax.numpy as jnp
from jax import lax
from jax.experimental import pallas as pl
from jax.experimental.pallas import tpu as pltpu

F32, BF16, I32 = jnp.float32, jnp.bfloat16, jnp.int32
_HIGHEST = lax.Precision.HIGHEST
_NT = (((1,), (1,)), ((), ()))
_LANES = 128
_SUBLANES = 8
_MASKED = -1e30
_BIAS_OFF = -32768.0
_LOG2E = 1.4426950408889634
_VMEM_LIMIT = 56 * 1024 * 1024
_DMA_UNROLL = 4

DEPTH = 4
FOX_HEADS, MOBA_HEADS, RET_HEADS = 8, 8, 4
MOBA_BLOCK, MOBA_TOPK = 256, 3
RET_ROPE_BASE, RET_GN_EPS = 10000.0, 1e-6
N_EXPERTS, TOP_K, MOE_BLOCK = 32, 4, 256
SWIGLU_LIMIT, SWIGLU_ALPHA = 7.0, 1.702
LN_EPS = 1e-5
DEEPNORM_ALPHA = (2 * DEPTH) ** 0.25


def _params(*sem):
    return pltpu.CompilerParams(dimension_semantics=sem, vmem_limit_bytes=_VMEM_LIMIT)


def _layer_norm(y, g, b):
    mu = jnp.mean(y, axis=-1, keepdims=True)
    d = y - mu
    var = jnp.mean(d * d, axis=-1, keepdims=True)
    return d * lax.rsqrt(var + LN_EPS) * g + b


def _store_slabs(ref, val):
    n = val.shape[0]
    for s in range(_SUBLANES):
        ref[pl.ds(s, n, stride=_SUBLANES), :] = val[:, s * _LANES:(s + 1) * _LANES]


def _load_slabs(ref, n):
    return jnp.concatenate([ref[pl.ds(s, n, stride=_SUBLANES), :] for s in range(_SUBLANES)], axis=1)


def _mm_kernel(a_ref, w_ref, o_ref):
    o_ref[...] = jnp.dot(a_ref[...], w_ref[...], preferred_element_type=F32).astype(o_ref.dtype)


def _matmul(a, w, out_dtype):
    m, k = a.shape
    n = w.shape[1]
    tm, tn = min(m, 1024), min(n, 1024)
    return pl.pallas_call(
        _mm_kernel,
        grid=(m // tm, n // tn),
        in_specs=[pl.BlockSpec((tm, k), lambda i, j: (i, 0)),
                  pl.BlockSpec((k, tn), lambda i, j: (0, j))],
        out_specs=pl.BlockSpec((tm, tn), lambda i, j: (i, j)),
        out_shape=jax.ShapeDtypeStruct((m, n), out_dtype),
        compiler_params=_params("parallel", "parallel"),
        name="proj_matmul",
    )(a, w)


def _flash_first(s, v):
    m = jnp.max(s, axis=1, keepdims=True)
    p = jnp.exp2(s - m)
    l = jnp.sum(p, axis=1, keepdims=True)
    acc = jnp.dot(p.astype(BF16), v, preferred_element_type=F32)
    return m, l, acc


def _flash_next(carry, s, v):
    m, l, acc = carry
    m_new = jnp.maximum(m, jnp.max(s, axis=1, keepdims=True))
    a = jnp.exp2(m - m_new)
    p = jnp.exp2(s - m_new)
    l = a * l + jnp.sum(p, axis=1, keepdims=True)
    acc = a * acc + jnp.dot(p.astype(BF16), v, preferred_element_type=F32)
    return m_new, l, acc


def _causal_flash(q_aug, kaug_ref, v_ref, i, tq):
    def tile(j):
        start = pl.multiple_of(j * tq, tq)
        s = lax.dot_general(q_aug, kaug_ref[pl.ds(start, tq), :], _NT, preferred_element_type=F32)
        return s, v_ref[pl.ds(start, tq), :]

    s, v = tile(i)
    row = lax.broadcasted_iota(I32, (tq, tq), 0)
    col = lax.broadcasted_iota(I32, (tq, tq), 1)
    carry = _flash_first(jnp.where(col <= row, s, _MASKED), v)
    _, l, acc = lax.fori_loop(0, i, lambda j, c: _flash_next(c, *tile(j)), carry)
    return acc / l


def _lane_columns(n, cols):
    lane = lax.broadcasted_iota(I32, (n, _LANES), 1)
    out = jnp.zeros((n, _LANES), F32)
    for t, c in enumerate(cols):
        out = jnp.where(lane == t, c, out)
    return out.astype(BF16)


def _split3(c):
    hi = c.astype(BF16).astype(F32)
    r = c - hi
    mid = r.astype(BF16).astype(F32)
    lo = (r - mid).astype(BF16).astype(F32)
    return [hi, mid, lo]


def _fox_gate_kernel(x_ref, wf_ref, bf_ref, c_ref, carry_ref):
    @pl.when(pl.program_id(1) == 0)
    def _():
        carry_ref[...] = jnp.zeros_like(carry_ref)

    z = jnp.dot(x_ref[...], wf_ref[...], precision=_HIGHEST, preferred_element_type=F32) + bf_ref[...]
    log_f = jnp.minimum(z, 0.0) - jnp.log1p(jnp.exp(-jnp.abs(z)))
    tc = z.shape[0]
    row = lax.broadcasted_iota(I32, (tc, tc), 0)
    col = lax.broadcasted_iota(I32, (tc, tc), 1)
    tri = (col <= row).astype(F32)
    c = jnp.dot(tri, log_f, precision=_HIGHEST, preferred_element_type=F32) + carry_ref[...]
    c_ref[...] = c
    carry_ref[...] = c[tc - 1:tc, :]


def _fox_gate_cumsum(x2, w_f, b_f, batch, seq):
    t, d = x2.shape
    h = w_f.shape[1]
    wf = jnp.zeros((d, _LANES), F32).at[:, :h].set(w_f)
    bf = jnp.zeros((1, _LANES), F32).at[0, :h].set(b_f)
    tc = min(seq, 512)
    ns = seq // tc
    return pl.pallas_call(
        _fox_gate_kernel,
        grid=(batch, ns),
        in_specs=[pl.BlockSpec((tc, d), lambda b, s: (b * ns + s, 0)),
                  pl.BlockSpec((d, _LANES), lambda b, s: (0, 0)),
                  pl.BlockSpec((1, _LANES), lambda b, s: (0, 0))],
        out_specs=pl.BlockSpec((tc, _LANES), lambda b, s: (b * ns + s, 0)),
        out_shape=jax.ShapeDtypeStruct((t, _LANES), F32),
        scratch_shapes=[pltpu.VMEM((1, _LANES), F32)],
        compiler_params=_params("parallel", "arbitrary"),
        name="fox_gate_cumsum",
    )(x2, wf, bf)


def _fox_attn_kernel(q_ref, k_ref, v_ref, cq_ref, ck_ref, o_ref, kaug_ref, *, scale):
    h = pl.program_id(1)
    i = pl.program_id(2)
    tq, hd = q_ref.shape
    seq = k_ref.shape[0]

    def head_column(c_ref, rows):
        lane = lax.broadcasted_iota(I32, (rows, _LANES), 1)
        return jnp.sum(jnp.where(lane == h, c_ref[...], 0.0), axis=1, keepdims=True) * _LOG2E

    @pl.when(i == 0)
    def _():
        kaug_ref[:, :hd] = k_ref[...]
        kaug_ref[:, hd:] = _lane_columns(seq, [1.0, 1.0, 1.0] + [-c for c in _split3(head_column(ck_ref, seq))])

    q = (q_ref[...].astype(F32) * (scale * _LOG2E)).astype(BF16)
    q_aug = jnp.concatenate([q, _lane_columns(tq, _split3(head_column(cq_ref, tq)) + [1.0, 1.0, 1.0])], axis=1)
    o_ref[...] = _causal_flash(q_aug, kaug_ref, v_ref, i, tq).astype(o_ref.dtype)


def _fox_attention(qkv, c_pad, batch, seq, heads):
    t = qkv.shape[0]
    d = qkv.shape[1] // 3
    hd = d // heads
    tq = min(seq, 512)
    nq = seq // tq
    return pl.pallas_call(
        functools.partial(_fox_attn_kernel, scale=hd ** -0.5),
        grid=(batch, heads, nq),
        in_specs=[pl.BlockSpec((tq, hd), lambda b, h, i: (b * nq + i, h)),
                  pl.BlockSpec((seq, hd), lambda b, h, i: (b, heads + h)),
                  pl.BlockSpec((seq, hd), lambda b, h, i: (b, 2 * heads + h)),
                  pl.BlockSpec((tq, _LANES), lambda b, h, i: (b * nq + i, 0)),
                  pl.BlockSpec((seq, _LANES), lambda b, h, i: (b, 0))],
        out_specs=pl.BlockSpec((tq, hd), lambda b, h, i: (b * nq + i, h)),
        out_shape=jax.ShapeDtypeStruct((t, d), BF16),
        scratch_shapes=[pltpu.VMEM((seq, hd + _LANES), BF16)],
        compiler_params=_params("parallel", "parallel", "arbitrary"),
        name="fox_attention",
    )(qkv, qkv, qkv, c_pad, c_pad)


def _moba_attn_kernel(q_ref, k_ref, v_ref, o_ref, kaug_ref, kmean_ref, *, scale, blk, topk):
    i = pl.program_id(2)
    tq, hd = q_ref.shape
    seq = k_ref.shape[0]
    nb = seq // blk

    @pl.when(i == 0)
    def _():
        kmean_ref[...] = jnp.zeros_like(kmean_ref)
        for n in range(nb):
            kb = k_ref[n * blk:(n + 1) * blk, :].astype(F32)
            kmean_ref[n:n + 1, :] = jnp.mean(kb, axis=0, keepdims=True)
        kaug_ref[:, :hd] = k_ref[...]
        key_block = lax.broadcasted_iota(I32, (seq, _LANES), 0) // blk
        lane = lax.broadcasted_iota(I32, (seq, _LANES), 1)
        kaug_ref[:, hd:] = jnp.where(key_block == lane, 1.0, 0.0).astype(BF16)

    q = q_ref[...]
    gate = lax.dot_general(q.astype(F32), kmean_ref[...], _NT, precision=_HIGHEST,
                           preferred_element_type=F32)
    lane = lax.broadcasted_iota(I32, gate.shape, 1)
    lane_f = lane.astype(F32)
    own = (i * tq + lax.broadcasted_iota(I32, gate.shape, 0)) // blk
    gate = jnp.where(lane < own, gate, -jnp.inf)
    allowed = lane == own
    for r in range(topk):
        mx = jnp.max(gate, axis=1, keepdims=True)
        idx = jnp.min(jnp.where(gate == mx, lane_f, float(_LANES)), axis=1, keepdims=True)
        pick = lane_f == idx
        allowed = jnp.logical_or(allowed, jnp.logical_and(pick, r < own))
        gate = jnp.where(pick, -jnp.inf, gate)
    bias = jnp.where(allowed, 0.0, _BIAS_OFF).astype(BF16)
    q_aug = jnp.concatenate([(q.astype(F32) * (scale * _LOG2E)).astype(BF16), bias], axis=1)
    o_ref[...] = _causal_flash(q_aug, kaug_ref, v_ref, i, tq).astype(o_ref.dtype)


def _moba_attention(qkv, batch, seq, heads):
    t = qkv.shape[0]
    d = qkv.shape[1] // 3
    hd = d // heads
    blk = MOBA_BLOCK
    assert seq % blk == 0 and seq // blk <= _LANES
    tq = 2 * blk if seq % (2 * blk) == 0 else blk
    nq = seq // tq
    return pl.pallas_call(
        functools.partial(_moba_attn_kernel, scale=hd ** -0.5, blk=blk, topk=min(MOBA_TOPK, seq // blk)),
        grid=(batch, heads, nq),
        in_specs=[pl.BlockSpec((tq, hd), lambda b, h, i: (b * nq + i, h)),
                  pl.BlockSpec((seq, hd), lambda b, h, i: (b, heads + h)),
                  pl.BlockSpec((seq, hd), lambda b, h, i: (b, 2 * heads + h))],
        out_specs=pl.BlockSpec((tq, hd), lambda b, h, i: (b * nq + i, h)),
        out_shape=jax.ShapeDtypeStruct((t, d), BF16),
        scratch_shapes=[pltpu.VMEM((seq, hd + _LANES), BF16), pltpu.VMEM((_LANES, hd), F32)],
        compiler_params=_params("parallel", "parallel", "arbitrary"),
        name="moba_attention",
    )(qkv, qkv, qkv)


def _retention_kernel(q_ref, k_ref, v_ref, g_ref, cos_ref, sin_ref, dm_ref, xi_ref, zeta_ref, gc_ref,
                      gn_ref, o_ref, state_ref, *, k_scale):
    @pl.when(pl.program_id(2) == 0)
    def _():
        state_ref[...] = jnp.zeros_like(state_ref)

    cos, sin = cos_ref[...], sin_ref[...]
    half = cos.shape[1]

    def rotate(t):
        t = t.astype(F32)
        t1, t2 = t[:, :half], t[:, half:]
        return jnp.concatenate([t1 * cos - t2 * sin, t1 * sin + t2 * cos], axis=1)

    q = rotate(q_ref[...]).astype(BF16)
    k_rot = rotate(k_ref[...]) * k_scale
    k = k_rot.astype(BF16)
    k_t = k_rot.T.astype(BF16)
    v = v_ref[...]
    s = lax.dot_general(q, k, _NT, preferred_element_type=F32) * dm_ref[0]
    inner = jnp.dot(s.astype(BF16), v, preferred_element_type=F32)
    state = state_ref[...]
    cross = jnp.dot(q, state.astype(BF16), preferred_element_type=F32) * xi_ref[0]
    o = inner + cross
    vz = (v.astype(F32) * zeta_ref[0]).astype(BF16)
    state_ref[...] = gc_ref[0][:, :1] * state + jnp.dot(k_t, vz, preferred_element_type=F32)

    mu = jnp.mean(o, axis=-1, keepdims=True)
    d = o - mu
    var = jnp.mean(d * d, axis=-1, keepdims=True)
    o = d * lax.rsqrt(var + RET_GN_EPS) * gn_ref[...]
    g = g_ref[...].astype(F32)
    o_ref[...] = (g / (1.0 + jnp.exp(-g)) * o).astype(o_ref.dtype)


def _retention(proj, gn_g, batch, seq, heads):
    t, width = proj.shape
    dk = width // (6 * heads)
    dv = 2 * dk
    chunk = min(seq, 256)
    nc = seq // chunk
    pos = jnp.arange(seq, dtype=F32)
    inv_freq = jnp.exp(-jnp.log(RET_ROPE_BASE) * jnp.arange(0, dk, 2, dtype=F32) / dk)
    ang = pos[:, None] * inv_freq[None, :]
    cos, sin = jnp.cos(ang), jnp.sin(ang)
    log_gamma = jnp.log(1.0 - jnp.exp2(-5.0 - jnp.arange(heads, dtype=F32)))
    n = jnp.arange(chunk, dtype=F32)
    diff = n[:, None] - n[None, :]
    d_mask = jnp.where(diff[None] >= 0, jnp.exp(diff[None] * log_gamma[:, None, None]), 0.0)
    xi = jnp.exp((n[None, :] + 1.0) * log_gamma[:, None])[:, :, None]
    zeta = jnp.exp((chunk - 1.0 - n[None, :]) * log_gamma[:, None])[:, :, None]
    g_chunk = jnp.broadcast_to(jnp.exp(chunk * log_gamma)[:, None, None], (heads, 1, _LANES))
    return pl.pallas_call(
        functools.partial(_retention_kernel, k_scale=dk ** -0.5),
        grid=(batch, heads, nc),
        in_specs=[pl.BlockSpec((chunk, dk), lambda b, h, c: (b * nc + c, h)),
                  pl.BlockSpec((chunk, dk), lambda b, h, c: (b * nc + c, heads + h)),
                  pl.BlockSpec((chunk, dv), lambda b, h, c: (b * nc + c, heads + h)),
                  pl.BlockSpec((chunk, dv), lambda b, h, c: (b * nc + c, 2 * heads + h)),
                  pl.BlockSpec((chunk, dk // 2), lambda b, h, c: (c, 0)),
                  pl.BlockSpec((chunk, dk // 2), lambda b, h, c: (c, 0)),
                  pl.BlockSpec((1, chunk, chunk), lambda b, h, c: (h, 0, 0)),
                  pl.BlockSpec((1, chunk, 1), lambda b, h, c: (h, 0, 0)),
                  pl.BlockSpec((1, chunk, 1), lambda b, h, c: (h, 0, 0)),
                  pl.BlockSpec((1, 1, _LANES), lambda b, h, c: (h, 0, 0)),
                  pl.BlockSpec((1, dv), lambda b, h, c: (0, h))],
        out_specs=pl.BlockSpec((chunk, dv), lambda b, h, c: (b * nc + c, h)),
        out_shape=jax.ShapeDtypeStruct((t, heads * dv), BF16),
        scratch_shapes=[pltpu.VMEM((dk, dv), F32)],
        compiler_params=_params("parallel", "parallel", "arbitrary"),
        name="retention",
    )(proj, proj, proj, proj, cos, sin, d_mask, xi, zeta, g_chunk, gn_g.reshape(1, -1))


def _post_mixer_kernel(o_ref, w_ref, x_ref, g_ref, b_ref, rw_ref, rb_ref,
                       x1s_ref, eidx_ref, gate_ref, rank_ref, cnt_ref, run_ref, *, n_exp, topk):
    @pl.when(pl.program_id(0) == 0)
    def _():
        run_ref[...] = jnp.zeros_like(run_ref)

    h = jnp.dot(o_ref[...], w_ref[...], preferred_element_type=F32)
    x1 = _layer_norm(DEEPNORM_ALPHA * x_ref[...] + h, g_ref[...], b_ref[...])
    _store_slabs(x1s_ref, x1)

    logits = jnp.dot(x1, rw_ref[...], precision=_HIGHEST, preferred_element_type=F32) + rb_ref[...]
    tm = logits.shape[0]
    lane = lax.broadcasted_iota(I32, logits.shape, 1).astype(F32)
    logits = jnp.where(lane < n_exp, logits, -jnp.inf)
    vals, idxs = [], []
    for _ in range(topk):
        mx = jnp.max(logits, axis=1, keepdims=True)
        idx = jnp.min(jnp.where(logits == mx, lane, float(_LANES)), axis=1, keepdims=True)
        vals.append(mx)
        idxs.append(idx)
        logits = jnp.where(lane == idx, -jnp.inf, logits)
    exps = [jnp.exp(v - vals[0]) for v in vals]
    den = exps[0]
    for e in exps[1:]:
        den = den + e

    picked = jnp.zeros(logits.shape, F32)
    for idx in idxs:
        picked = jnp.where(lane == idx, 1.0, picked)
    row = lax.broadcasted_iota(I32, (tm, tm), 0)
    col = lax.broadcasted_iota(I32, (tm, tm), 1)
    before = (col < row).astype(BF16)
    prefix = jnp.dot(before, picked.astype(BF16), preferred_element_type=F32) + run_ref[...]

    eidx = jnp.zeros(logits.shape, F32)
    gates = jnp.zeros(logits.shape, F32)
    ranks = jnp.zeros(logits.shape, F32)
    for r in range(topk):
        rank_r = jnp.sum(jnp.where(lane == idxs[r], prefix, 0.0), axis=1, keepdims=True)
        eidx = jnp.where(lane == r, idxs[r], eidx)
        gates = jnp.where(lane == r, exps[r] / den, gates)
        ranks = jnp.where(lane == r, rank_r, ranks)
    eidx_ref[...] = eidx.astype(I32)
    gate_ref[...] = gates
    rank_ref[...] = ranks.astype(I32)
    run = run_ref[...] + jnp.sum(picked, axis=0, keepdims=True)
    run_ref[...] = run
    cnt_ref[...] = jnp.broadcast_to(run, cnt_ref.shape).astype(I32)


def _post_mixer(o, w_out, x2, ln_g, ln_b, router_w, router_b):
    t, d = x2.shape
    assert d == _SUBLANES * _LANES
    kdim = o.shape[1]
    n_exp = router_w.shape[1]
    tm = min(t, 512)
    rw = jnp.zeros((d, _LANES), F32).at[:, :n_exp].set(router_w)
    rb = jnp.zeros((1, _LANES), F32).at[0, :n_exp].set(router_b)
    row = lambda i: (i, 0)
    fixed = lambda i: (0, 0)
    return pl.pallas_call(
        functools.partial(_post_mixer_kernel, n_exp=n_exp, topk=TOP_K),
        grid=(t // tm,),
        in_specs=[pl.BlockSpec((tm, kdim), row), pl.BlockSpec((kdim, d), fixed),
                  pl.BlockSpec((tm, d), row), pl.BlockSpec((1, d), fixed), pl.BlockSpec((1, d), fixed),
                  pl.BlockSpec((d, _LANES), fixed), pl.BlockSpec((1, _LANES), fixed)],
        out_specs=[pl.BlockSpec((tm * _SUBLANES, _LANES), row), pl.BlockSpec((tm, _LANES), row),
                   pl.BlockSpec((tm, _LANES), row), pl.BlockSpec((tm, _LANES), row),
                   pl.BlockSpec((8, _LANES), fixed)],
        out_shape=[jax.ShapeDtypeStruct((t * _SUBLANES, _LANES), F32), jax.ShapeDtypeStruct((t, _LANES), I32),
                   jax.ShapeDtypeStruct((t, _LANES), F32), jax.ShapeDtypeStruct((t, _LANES), I32),
                   jax.ShapeDtypeStruct((8, _LANES), I32)],
        scratch_shapes=[pltpu.VMEM((1, _LANES), F32)],
        compiler_params=_params("arbitrary"),
        name="post_mixer",
    )(o, w_out, x2, ln_g.reshape(1, d), ln_b.reshape(1, d), rw, rb)


def _slab(ref, r):
    return ref.at[pl.ds(pl.multiple_of(r * _SUBLANES, _SUBLANES), _SUBLANES)]


def _dispatch_kernel(dest_ref, x_ref, xs_in_ref, xs_ref, sem, *, rows):
    del xs_in_ref
    base = pl.program_id(0) * rows * TOP_K

    def start(t, c):
        for k in range(TOP_K):
            pltpu.make_async_copy(_slab(x_ref, t), _slab(xs_ref, dest_ref[base + t * TOP_K + k]), sem).start()
        return c

    lax.fori_loop(0, rows, start, 0, unroll=_DMA_UNROLL)
    for _ in range(TOP_K):
        pltpu.make_async_copy(x_ref, xs_ref.at[pl.ds(0, rows * _SUBLANES)], sem).wait()


def _dispatch(dest_flat, x1s, n_rows):
    t = x1s.shape[0] // _SUBLANES
    rows = min(t, 256)
    xs0 = jnp.zeros((n_rows * _SUBLANES, _LANES), x1s.dtype)
    return pl.pallas_call(
        functools.partial(_dispatch_kernel, rows=rows),
        grid_spec=pltpu.PrefetchScalarGridSpec(
            num_scalar_prefetch=1,
            grid=(t // rows,),
            in_specs=[pl.BlockSpec((rows * _SUBLANES, _LANES), lambda i, dest: (i, 0)),
                      pl.BlockSpec(memory_space=pl.ANY)],
            out_specs=pl.BlockSpec(memory_space=pl.ANY),
            scratch_shapes=[pltpu.SemaphoreType.DMA(())]),
        out_shape=jax.ShapeDtypeStruct(xs0.shape, xs0.dtype),
        input_output_aliases={2: 0},
        compiler_params=_params("arbitrary"),
        name="moe_dispatch",
    )(dest_flat, x1s, xs0)


def _ffn_kernel(blk_e_ref, n_used_ref, xs_ref, w1_ref, b1_ref, w2_ref, b2_ref, ys_ref, w1b_ref, w2b_ref):
    i = pl.program_id(0)
    e = blk_e_ref[i]
    e_prev = blk_e_ref[jnp.maximum(i - 1, 0)]

    @pl.when(jnp.logical_or(i == 0, e != e_prev))
    def _():
        w1b_ref[...] = w1_ref[0, 0].astype(BF16)
        w2b_ref[...] = w2_ref[0, 0].astype(BF16)

    @pl.when(i < n_used_ref[0])
    def _():
        f = w2b_ref.shape[0]
        x = _load_slabs(xs_ref, MOE_BLOCK).astype(BF16)
        h = jnp.dot(x, w1b_ref[...], preferred_element_type=F32) + b1_ref[0, 0]
        glu = jnp.minimum(h[:, :f], SWIGLU_LIMIT)
        lin = jnp.clip(h[:, f:], -SWIGLU_LIMIT, SWIGLU_LIMIT)
        act = glu / (1.0 + jnp.exp(-SWIGLU_ALPHA * glu)) * (lin + 1.0)
        _store_slabs(ys_ref, jnp.dot(act.astype(BF16), w2b_ref[...], preferred_element_type=F32) + b2_ref[0, 0])

    @pl.when(i >= n_used_ref[0])
    def _():
        ys_ref[...] = jnp.zeros_like(ys_ref)


def _expert_ffn(blk_e, n_used, xs, layer, w1, b1, w2, b2):
    depth, n_exp, d, f2 = w1.shape
    f = w2.shape[2]
    n_blocks = xs.shape[0] // (MOE_BLOCK * _SUBLANES)
    by_expert = lambda i, be, nu: (layer, be[i], 0, 0)
    rows = pl.BlockSpec((MOE_BLOCK * _SUBLANES, _LANES), lambda i, be, nu: (i, 0))
    return pl.pallas_call(
        _ffn_kernel,
        grid_spec=pltpu.PrefetchScalarGridSpec(
            num_scalar_prefetch=2,
            grid=(n_blocks,),
            in_specs=[rows,
                      pl.BlockSpec((1, 1, d, f2), by_expert), pl.BlockSpec((1, 1, 1, f2), by_expert),
                      pl.BlockSpec((1, 1, f, d), by_expert), pl.BlockSpec((1, 1, 1, d), by_expert)],
            out_specs=rows,
            scratch_shapes=[pltpu.VMEM((d, f2), BF16), pltpu.VMEM((f, d), BF16)]),
        out_shape=jax.ShapeDtypeStruct(xs.shape, F32),
        compiler_params=_params("arbitrary"),
        name="moe_expert_ffn",
    )(blk_e, n_used, xs, w1, b1.reshape(depth, n_exp, 1, f2), w2, b2.reshape(depth, n_exp, 1, d))


def _combine_kernel(dest_ref, ys_ref, gate_ref, x1s_ref, g_ref, b_ref, x2_ref, x2b_ref, buf_ref, sem, *, rows):
    base = pl.program_id(0) * rows * TOP_K

    def start(t, c):
        for k in range(TOP_K):
            pltpu.make_async_copy(_slab(ys_ref, dest_ref[base + t * TOP_K + k]), _slab(buf_ref.at[k], t), sem).start()
        return c

    lax.fori_loop(0, rows, start, 0, unroll=_DMA_UNROLL)
    pltpu.make_async_copy(buf_ref, buf_ref, sem).wait()
    y = gate_ref[:, 0:1] * _load_slabs(buf_ref.at[0], rows)
    for k in range(1, TOP_K):
        y = y + gate_ref[:, k:k + 1] * _load_slabs(buf_ref.at[k], rows)
    x2 = _layer_norm(DEEPNORM_ALPHA * _load_slabs(x1s_ref, rows) + y, g_ref[...], b_ref[...])
    x2_ref[...] = x2
    x2b_ref[...] = x2.astype(BF16)


def _combine(dest_flat, ys, gates, x1s, ln_g, ln_b):
    t = gates.shape[0]
    d = _SUBLANES * _LANES
    rows = min(t, 256)
    row = lambda i, dest: (i, 0)
    fixed = lambda i, dest: (0, 0)
    return pl.pallas_call(
        functools.partial(_combine_kernel, rows=rows),
        grid_spec=pltpu.PrefetchScalarGridSpec(
            num_scalar_prefetch=1,
            grid=(t // rows,),
            in_specs=[pl.BlockSpec(memory_space=pl.ANY), pl.BlockSpec((rows, _LANES), row),
                      pl.BlockSpec((rows * _SUBLANES, _LANES), row),
                      pl.BlockSpec((1, d), fixed), pl.BlockSpec((1, d), fixed)],
            out_specs=[pl.BlockSpec((rows, d), row), pl.BlockSpec((rows, d), row)],
            scratch_shapes=[pltpu.VMEM((TOP_K, rows * _SUBLANES, _LANES), F32), pltpu.SemaphoreType.DMA(())]),
        out_shape=[jax.ShapeDtypeStruct((t, d), F32), jax.ShapeDtypeStruct((t, d), BF16)],
        compiler_params=_params("arbitrary"),
        name="moe_combine",
    )(dest_flat, ys, gates, x1s, ln_g.reshape(1, d), ln_b.reshape(1, d))


def _moe_layer(x1s, eidx, gates, rank, counts, layer, w1, b1, w2, b2, ln_g, ln_b):
    t = gates.shape[0]
    n_exp = w1.shape[1]
    n_blocks = -(-(t * TOP_K) // MOE_BLOCK) + n_exp
    cnt = counts[0, :n_exp]
    padded = ((cnt + MOE_BLOCK - 1) // MOE_BLOCK) * MOE_BLOCK
    pad_ends = jnp.cumsum(padded)
    pad_starts = pad_ends - padded
    dest = (pad_starts[eidx[:, :TOP_K]] + rank[:, :TOP_K]).astype(I32).reshape(-1)
    blk_start = jnp.arange(n_blocks, dtype=I32) * MOE_BLOCK
    blk_e = jnp.minimum(jnp.sum(pad_ends[None, :] <= blk_start[:, None], axis=1), n_exp - 1).astype(I32)
    n_used = (pad_ends[-1:] // MOE_BLOCK).astype(I32)
    xs = _dispatch(dest, x1s, n_blocks * MOE_BLOCK)
    ys = _expert_ffn(blk_e, n_used, xs, layer, w1, b1, w2, b2)
    return _combine(dest, ys, gates, x1s, ln_g, ln_b)


def _fox_mixer(x2, xb, w_in, b_f, batch, seq):
    d = x2.shape[1]
    qkv = _matmul(xb, w_in[:, :3 * d].astype(BF16), BF16)
    c_pad = _fox_gate_cumsum(x2, w_in[:, 3 * d:], b_f, batch, seq)
    return _fox_attention(qkv, c_pad, batch, seq, FOX_HEADS)


def _moba_mixer(xb, w_in, batch, seq):
    return _moba_attention(_matmul(xb, w_in.astype(BF16), BF16), batch, seq, MOBA_HEADS)


def _retention_mixer(xb, w_in, gn_g, batch, seq):
    return _retention(_matmul(xb, w_in.astype(BF16), BF16), gn_g, batch, seq, RET_HEADS)


def kernel(x, fox_w_in, fox_b_f, fox_w_out, moba_w_in, moba_w_out, ret_w_in, ret_gn_g, ret_w_out, ln_g, ln_b,
           router_w, router_b, moe_w1, moe_b1, moe_w2, moe_b2):
    batch, seq, d = x.shape
    x2 = x.reshape(batch * seq, d)
    xb = x2.astype(BF16)
    for i in range(DEPTH):
        kind, j = i % 3, i // 3
        if kind == 0:
            o, w_out = _fox_mixer(x2, xb, fox_w_in[j], fox_b_f[j], batch, seq), fox_w_out[j]
        elif kind == 1:
            o, w_out = _moba_mixer(xb, moba_w_in[j], batch, seq), moba_w_out[j]
        else:
            o, w_out = _retention_mixer(xb, ret_w_in[j], ret_gn_g[j], batch, seq), ret_w_out[j]
        x1s, eidx, gates, rank, counts = _post_mixer(o, w_out.astype(BF16), x2, ln_g[i, 0], ln_b[i, 0],
                                                     router_w[i], router_b[i])
        x2, xb = _moe_layer(x1s, eidx, gates, rank, counts, i, moe_w1, moe_b1, moe_w2, moe_b2,
                            ln_g[i, 1], ln_b[i, 1])
    return x2.reshape(batch, seq, d)
```

```python
import functools

import jax
import jax.numpy as jnp
from jax import lax
from jax.experimental import pallas as pl
from jax.experimental.pallas import tpu as pltpu

F32, BF16, I32 = jnp.float32, jnp.bfloat16, jnp.int32
_HIGHEST = lax.Precision.HIGHEST
_NT = (((1,), (1,)), ((), ()))
_LANES = 128
_SUBLANES = 8
_MASKED = -1e30
_BIAS_OFF = -32768.0
_LOG2E = 1.4426950408889634
_VMEM_LIMIT = 56 * 1024 * 1024
_MOE_TILE = 512
_RUN_CHUNK = 16

DEPTH = 4
FOX_HEADS, MOBA_HEADS, RET_HEADS = 8, 8, 4
MOBA_BLOCK, MOBA_TOPK = 256, 3
RET_ROPE_BASE, RET_GN_EPS = 10000.0, 1e-6
N_EXPERTS, TOP_K, MOE_BLOCK = 32, 4, 256
SWIGLU_LIMIT, SWIGLU_ALPHA = 7.0, 1.702
LN_EPS = 1e-5
DEEPNORM_ALPHA = (2 * DEPTH) ** 0.25


def _params(*sem):
    return pltpu.CompilerParams(dimension_semantics=sem, vmem_limit_bytes=_VMEM_LIMIT)


def _layer_norm(y, g, b):
    mu = jnp.mean(y, axis=-1, keepdims=True)
    d = y - mu
    var = jnp.mean(d * d, axis=-1, keepdims=True)
    return d * lax.rsqrt(var + LN_EPS) * g + b


def _store_slabs(ref, val):
    n = val.shape[0]
    for s in range(_SUBLANES):
        ref[pl.ds(s, n, stride=_SUBLANES), :] = val[:, s * _LANES:(s + 1) * _LANES]


def _load_slabs(ref, n):
    return jnp.concatenate([ref[pl.ds(s, n, stride=_SUBLANES), :] for s in range(_SUBLANES)], axis=1)


def _mm_kernel(a_ref, w_ref, o_ref):
    o_ref[...] = jnp.dot(a_ref[...], w_ref[...], preferred_element_type=F32).astype(o_ref.dtype)


def _matmul(a, w, out_dtype):
    m, k = a.shape
    n = w.shape[1]
    tm, tn = min(m, 1024), min(n, 1024)
    return pl.pallas_call(
        _mm_kernel,
        grid=(m // tm, n // tn),
        in_specs=[pl.BlockSpec((tm, k), lambda i, j: (i, 0)),
                  pl.BlockSpec((k, tn), lambda i, j: (0, j))],
        out_specs=pl.BlockSpec((tm, tn), lambda i, j: (i, j)),
        out_shape=jax.ShapeDtypeStruct((m, n), out_dtype),
        compiler_params=_params("parallel", "parallel"),
        name="proj_matmul",
    )(a, w)


def _flash_first(s, v):
    m = jnp.max(s, axis=1, keepdims=True)
    p = jnp.exp2(s - m)
    l = jnp.sum(p, axis=1, keepdims=True)
    acc = jnp.dot(p.astype(BF16), v, preferred_element_type=F32)
    return m, l, acc


def _flash_next(carry, s, v):
    m, l, acc = carry
    m_new = jnp.maximum(m, jnp.max(s, axis=1, keepdims=True))
    a = jnp.exp2(m - m_new)
    p = jnp.exp2(s - m_new)
    l = a * l + jnp.sum(p, axis=1, keepdims=True)
    acc = a * acc + jnp.dot(p.astype(BF16), v, preferred_element_type=F32)
    return m_new, l, acc


def _causal_flash(q_aug, kaug_ref, v_ref, i, tq):
    def tile(j, width):
        start = pl.multiple_of(j * tq, tq)
        s = lax.dot_general(q_aug, kaug_ref[pl.ds(start, width), :], _NT, preferred_element_type=F32)
        return s, v_ref[pl.ds(start, width), :]

    s, v = tile(i, tq)
    row = lax.broadcasted_iota(I32, (tq, tq), 0)
    col = lax.broadcasted_iota(I32, (tq, tq), 1)
    carry = _flash_first(jnp.where(col <= row, s, _MASKED), v)
    carry = lax.fori_loop(0, i // 2, lambda p, c: _flash_next(c, *tile(2 * p, 2 * tq)), carry)
    _, l, acc = lax.cond(i % 2 == 1, lambda c: _flash_next(c, *tile(i - 1, tq)), lambda c: c, carry)
    return acc / l


def _lane_columns(n, cols):
    lane = lax.broadcasted_iota(I32, (n, _LANES), 1)
    out = jnp.zeros((n, _LANES), F32)
    for t, c in enumerate(cols):
        out = jnp.where(lane == t, c, out)
    return out.astype(BF16)


def _split3(c):
    hi = c.astype(BF16).astype(F32)
    r = c - hi
    mid = r.astype(BF16).astype(F32)
    lo = (r - mid).astype(BF16).astype(F32)
    return [hi, mid, lo]


def _fox_gate_kernel(x_ref, wf_ref, bf_ref, c_ref, carry_ref):
    @pl.when(pl.program_id(1) == 0)
    def _():
        carry_ref[...] = jnp.zeros_like(carry_ref)

    z = jnp.dot(x_ref[...], wf_ref[...], precision=_HIGHEST, preferred_element_type=F32) + bf_ref[...]
    log_f = jnp.minimum(z, 0.0) - jnp.log1p(jnp.exp(-jnp.abs(z)))
    tc = z.shape[0]
    row = lax.broadcasted_iota(I32, (tc, tc), 0)
    col = lax.broadcasted_iota(I32, (tc, tc), 1)
    tri = (col <= row).astype(F32)
    c = jnp.dot(tri, log_f, precision=_HIGHEST, preferred_element_type=F32) + carry_ref[...]
    c_ref[...] = c
    carry_ref[...] = c[tc - 1:tc, :]


def _fox_gate_cumsum(x2, w_f, b_f, batch, seq):
    t, d = x2.shape
    h = w_f.shape[1]
    wf = jnp.zeros((d, _LANES), F32).at[:, :h].set(w_f)
    bf = jnp.zeros((1, _LANES), F32).at[0, :h].set(b_f)
    tc = min(seq, 512)
    ns = seq // tc
    return pl.pallas_call(
        _fox_gate_kernel,
        grid=(batch, ns),
        in_specs=[pl.BlockSpec((tc, d), lambda b, s: (b * ns + s, 0)),
                  pl.BlockSpec((d, _LANES), lambda b, s: (0, 0)),
                  pl.BlockSpec((1, _LANES), lambda b, s: (0, 0))],
        out_specs=pl.BlockSpec((tc, _LANES), lambda b, s: (b * ns + s, 0)),
        out_shape=jax.ShapeDtypeStruct((t, _LANES), F32),
        scratch_shapes=[pltpu.VMEM((1, _LANES), F32)],
        compiler_params=_params("parallel", "arbitrary"),
        name="fox_gate_cumsum",
    )(x2, wf, bf)


def _fox_attn_kernel(q_ref, k_ref, v_ref, cq_ref, ck_ref, o_ref, kaug_ref, *, scale):
    h = pl.program_id(1)
    i = pl.program_id(2)
    tq, hd = q_ref.shape
    seq = k_ref.shape[0]

    def head_column(c_ref, rows):
        lane = lax.broadcasted_iota(I32, (rows, _LANES), 1)
        return jnp.sum(jnp.where(lane == h, c_ref[...], 0.0), axis=1, keepdims=True) * _LOG2E

    @pl.when(i == 0)
    def _():
        kaug_ref[:, :hd] = k_ref[...]
        kaug_ref[:, hd:] = _lane_columns(seq, [1.0, 1.0, 1.0] + [-c for c in _split3(head_column(ck_ref, seq))])

    q = (q_ref[...].astype(F32) * (scale * _LOG2E)).astype(BF16)
    q_aug = jnp.concatenate([q, _lane_columns(tq, _split3(head_column(cq_ref, tq)) + [1.0, 1.0, 1.0])], axis=1)
    o_ref[...] = _causal_flash(q_aug, kaug_ref, v_ref, i, tq).astype(o_ref.dtype)


def _fox_attention(qkv, c_pad, batch, seq, heads):
    t = qkv.shape[0]
    d = qkv.shape[1] // 3
    hd = d // heads
    tq = min(seq, 512)
    nq = seq // tq
    return pl.pallas_call(
        functools.partial(_fox_attn_kernel, scale=hd ** -0.5),
        grid=(batch, heads, nq),
        in_specs=[pl.BlockSpec((tq, hd), lambda b, h, i: (b * nq + i, h)),
                  pl.BlockSpec((seq, hd), lambda b, h, i: (b, heads + h)),
                  pl.BlockSpec((seq, hd), lambda b, h, i: (b, 2 * heads + h)),
                  pl.BlockSpec((tq, _LANES), lambda b, h, i: (b * nq + i, 0)),
                  pl.BlockSpec((seq, _LANES), lambda b, h, i: (b, 0))],
        out_specs=pl.BlockSpec((tq, hd), lambda b, h, i: (b * nq + i, h)),
        out_shape=jax.ShapeDtypeStruct((t, d), BF16),
        scratch_shapes=[pltpu.VMEM((seq, hd + _LANES), BF16)],
        compiler_params=_params("parallel", "parallel", "arbitrary"),
        name="fox_attention",
    )(qkv, qkv, qkv, c_pad, c_pad)


def _moba_attn_kernel(q_ref, k_ref, v_ref, o_ref, kaug_ref, kmean_ref, *, scale, blk, topk):
    i = pl.program_id(2)
    tq, hd = q_ref.shape
    seq = k_ref.shape[0]
    nb = seq // blk

    @pl.when(i == 0)
    def _():
        kmean_ref[...] = jnp.zeros_like(kmean_ref)
        for n in range(nb):
            kb = k_ref[n * blk:(n + 1) * blk, :].astype(F32)
            kmean_ref[n:n + 1, :] = jnp.mean(kb, axis=0, keepdims=True)
        kaug_ref[:, :hd] = k_ref[...]
        key_block = lax.broadcasted_iota(I32, (seq, _LANES), 0) // blk
        lane = lax.broadcasted_iota(I32, (seq, _LANES), 1)
        kaug_ref[:, hd:] = jnp.where(key_block == lane, 1.0, 0.0).astype(BF16)

    q = q_ref[...]
    gate = lax.dot_general(q.astype(F32), kmean_ref[...], _NT, precision=_HIGHEST,
                           preferred_element_type=F32)
    lane = lax.broadcasted_iota(I32, gate.shape, 1)
    lane_f = lane.astype(F32)
    own = (i * tq + lax.broadcasted_iota(I32, gate.shape, 0)) // blk
    gate = jnp.where(lane < own, gate, -jnp.inf)
    allowed = lane == own
    for r in range(topk):
        mx = jnp.max(gate, axis=1, keepdims=True)
        idx = jnp.min(jnp.where(gate == mx, lane_f, float(_LANES)), axis=1, keepdims=True)
        pick = lane_f == idx
        allowed = jnp.logical_or(allowed, jnp.logical_and(pick, r < own))
        gate = jnp.where(pick, -jnp.inf, gate)
    bias = jnp.where(allowed, 0.0, _BIAS_OFF).astype(BF16)
    q_aug = jnp.concatenate([(q.astype(F32) * (scale * _LOG2E)).astype(BF16), bias], axis=1)
    o_ref[...] = _causal_flash(q_aug, kaug_ref, v_ref, i, tq).astype(o_ref.dtype)


def _moba_attention(qkv, batch, seq, heads):
    t = qkv.shape[0]
    d = qkv.shape[1] // 3
    hd = d // heads
    blk = MOBA_BLOCK
    assert seq % blk == 0 and seq // blk <= _LANES
    tq = 2 * blk if seq % (2 * blk) == 0 else blk
    nq = seq // tq
    return pl.pallas_call(
        functools.partial(_moba_attn_kernel, scale=hd ** -0.5, blk=blk, topk=min(MOBA_TOPK, seq // blk)),
        grid=(batch, heads, nq),
        in_specs=[pl.BlockSpec((tq, hd), lambda b, h, i: (b * nq + i, h)),
                  pl.BlockSpec((seq, hd), lambda b, h, i: (b, heads + h)),
                  pl.BlockSpec((seq, hd), lambda b, h, i: (b, 2 * heads + h))],
        out_specs=pl.BlockSpec((tq, hd), lambda b, h, i: (b * nq + i, h)),
        out_shape=jax.ShapeDtypeStruct((t, d), BF16),
        scratch_shapes=[pltpu.VMEM((seq, hd + _LANES), BF16), pltpu.VMEM((_LANES, hd), F32)],
        compiler_params=_params("parallel", "parallel", "arbitrary"),
        name="moba_attention",
    )(qkv, qkv, qkv)


def _retention_kernel(q_ref, k_ref, v_ref, g_ref, cos_ref, sin_ref, dm_ref, xi_ref, zeta_ref, gc_ref,
                      gn_ref, o_ref, state_ref, *, k_scale):
    @pl.when(pl.program_id(2) == 0)
    def _():
        state_ref[...] = jnp.zeros_like(state_ref)

    cos, sin = cos_ref[...], sin_ref[...]
    half = cos.shape[1]

    def rotate(t):
        t = t.astype(F32)
        t1, t2 = t[:, :half], t[:, half:]
        return jnp.concatenate([t1 * cos - t2 * sin, t1 * sin + t2 * cos], axis=1)

    q = rotate(q_ref[...]).astype(BF16)
    k_rot = rotate(k_ref[...]) * k_scale
    k = k_rot.astype(BF16)
    k_t = k_rot.T.astype(BF16)
    v = v_ref[...]
    s = lax.dot_general(q, k, _NT, preferred_element_type=F32) * dm_ref[0]
    inner = jnp.dot(s.astype(BF16), v, preferred_element_type=F32)
    state = state_ref[...]
    cross = jnp.dot(q, state.astype(BF16), preferred_element_type=F32) * xi_ref[0]
    o = inner + cross
    vz = (v.astype(F32) * zeta_ref[0]).astype(BF16)
    state_ref[...] = gc_ref[0][:, :1] * state + jnp.dot(k_t, vz, preferred_element_type=F32)

    mu = jnp.mean(o, axis=-1, keepdims=True)
    d = o - mu
    var = jnp.mean(d * d, axis=-1, keepdims=True)
    o = d * lax.rsqrt(var + RET_GN_EPS) * gn_ref[...]
    g = g_ref[...].astype(F32)
    o_ref[...] = (g / (1.0 + jnp.exp(-g)) * o).astype(o_ref.dtype)


def _retention(proj, gn_g, batch, seq, heads):
    t, width = proj.shape
    dk = width // (6 * heads)
    dv = 2 * dk
    chunk = min(seq, 256)
    nc = seq // chunk
    pos = jnp.arange(seq, dtype=F32)
    inv_freq = jnp.exp(-jnp.log(RET_ROPE_BASE) * jnp.arange(0, dk, 2, dtype=F32) / dk)
    ang = pos[:, None] * inv_freq[None, :]
    cos, sin = jnp.cos(ang), jnp.sin(ang)
    log_gamma = jnp.log(1.0 - jnp.exp2(-5.0 - jnp.arange(heads, dtype=F32)))
    n = jnp.arange(chunk, dtype=F32)
    diff = n[:, None] - n[None, :]
    d_mask = jnp.where(diff[None] >= 0, jnp.exp(diff[None] * log_gamma[:, None, None]), 0.0)
    xi = jnp.exp((n[None, :] + 1.0) * log_gamma[:, None])[:, :, None]
    zeta = jnp.exp((chunk - 1.0 - n[None, :]) * log_gamma[:, None])[:, :, None]
    g_chunk = jnp.broadcast_to(jnp.exp(chunk * log_gamma)[:, None, None], (heads, 1, _LANES))
    return pl.pallas_call(
        functools.partial(_retention_kernel, k_scale=dk ** -0.5),
        grid=(batch, heads, nc),
        in_specs=[pl.BlockSpec((chunk, dk), lambda b, h, c: (b * nc + c, h)),
                  pl.BlockSpec((chunk, dk), lambda b, h, c: (b * nc + c, heads + h)),
                  pl.BlockSpec((chunk, dv), lambda b, h, c: (b * nc + c, heads + h)),
                  pl.BlockSpec((chunk, dv), lambda b, h, c: (b * nc + c, 2 * heads + h)),
                  pl.BlockSpec((chunk, dk // 2), lambda b, h, c: (c, 0)),
                  pl.BlockSpec((chunk, dk // 2), lambda b, h, c: (c, 0)),
                  pl.BlockSpec((1, chunk, chunk), lambda b, h, c: (h, 0, 0)),
                  pl.BlockSpec((1, chunk, 1), lambda b, h, c: (h, 0, 0)),
                  pl.BlockSpec((1, chunk, 1), lambda b, h, c: (h, 0, 0)),
                  pl.BlockSpec((1, 1, _LANES), lambda b, h, c: (h, 0, 0)),
                  pl.BlockSpec((1, dv), lambda b, h, c: (0, h))],
        out_specs=pl.BlockSpec((chunk, dv), lambda b, h, c: (b * nc + c, h)),
        out_shape=jax.ShapeDtypeStruct((t, heads * dv), BF16),
        scratch_shapes=[pltpu.VMEM((dk, dv), F32)],
        compiler_params=_params("parallel", "parallel", "arbitrary"),
        name="retention",
    )(proj, proj, proj, proj, cos, sin, d_mask, xi, zeta, g_chunk, gn_g.reshape(1, -1))


def _post_mixer_kernel(o_ref, w_ref, x_ref, g_ref, b_ref, rw_ref, rb_ref,
                       x1_ref, x1b_ref, eidx_ref, gate_ref, rank_ref, start_ref, cnt_ref, run_ref, *, n_exp, topk):
    @pl.when(pl.program_id(0) == 0)
    def _():
        run_ref[...] = jnp.zeros_like(run_ref)

    h = jnp.dot(o_ref[...], w_ref[...], preferred_element_type=F32)
    x1 = _layer_norm(DEEPNORM_ALPHA * x_ref[...] + h, g_ref[...], b_ref[...])
    x1_ref[...] = x1
    x1b_ref[...] = x1.astype(BF16)
    start_ref[...] = jnp.broadcast_to(run_ref[...], start_ref.shape[1:]).astype(I32)[None]

    x_hi = x1.astype(BF16)
    x_lo = (x1 - x_hi.astype(F32)).astype(BF16)
    hi_part = jnp.dot(x_hi, rw_ref[...], preferred_element_type=F32)
    lo_part = jnp.dot(x_lo, rw_ref[:, :_LANES], preferred_element_type=F32)
    logits = hi_part[:, :_LANES] + hi_part[:, _LANES:] + lo_part + rb_ref[...]
    tm = logits.shape[0]
    lane = lax.broadcasted_iota(I32, logits.shape, 1).astype(F32)
    logits = jnp.where(lane < n_exp, logits, -jnp.inf)
    vals, idxs = [], []
    for _ in range(topk):
        mx = jnp.max(logits, axis=1, keepdims=True)
        idx = jnp.min(jnp.where(logits == mx, lane, float(_LANES)), axis=1, keepdims=True)
        vals.append(mx)
        idxs.append(idx)
        logits = jnp.where(lane == idx, -jnp.inf, logits)
    exps = [jnp.exp(v - vals[0]) for v in vals]
    den = exps[0]
    for e in exps[1:]:
        den = den + e

    picked = jnp.zeros(logits.shape, F32)
    for idx in idxs:
        picked = jnp.where(lane == idx, 1.0, picked)
    row = lax.broadcasted_iota(I32, (tm, tm), 0)
    col = lax.broadcasted_iota(I32, (tm, tm), 1)
    before = (col < row).astype(BF16)
    prefix = jnp.dot(before, picked.astype(BF16), preferred_element_type=F32)

    eidx = jnp.zeros(logits.shape, F32)
    gates = jnp.zeros(logits.shape, F32)
    ranks = jnp.zeros(logits.shape, F32)
    for r in range(topk):
        rank_r = jnp.sum(jnp.where(lane == idxs[r], prefix, 0.0), axis=1, keepdims=True)
        eidx = jnp.where(lane == r, idxs[r], eidx)
        gates = jnp.where(lane == r, exps[r] / den, gates)
        ranks = jnp.where(lane == r, rank_r, ranks)
    eidx_ref[...] = eidx.astype(I32)
    gate_ref[...] = gates
    rank_ref[...] = ranks.astype(I32)
    run = run_ref[...] + jnp.sum(picked, axis=0, keepdims=True)
    run_ref[...] = run
    cnt_ref[...] = jnp.broadcast_to(run, cnt_ref.shape).astype(I32)


def _post_mixer(o, w_out, x2, ln_g, ln_b, router_w, router_b):
    t, d = x2.shape
    assert d == _SUBLANES * _LANES
    kdim = o.shape[1]
    n_exp = router_w.shape[1]
    tm = min(t, _MOE_TILE)
    rw = jnp.zeros((d, _LANES), F32).at[:, :n_exp].set(router_w)
    rw_hi = rw.astype(BF16)
    rw = jnp.concatenate([rw_hi, (rw - rw_hi.astype(F32)).astype(BF16)], axis=1)
    rb = jnp.zeros((1, _LANES), F32).at[0, :n_exp].set(router_b)
    row = lambda i: (i, 0)
    fixed = lambda i: (0, 0)
    return pl.pallas_call(
        functools.partial(_post_mixer_kernel, n_exp=n_exp, topk=TOP_K),
        grid=(t // tm,),
        in_specs=[pl.BlockSpec((tm, kdim), row), pl.BlockSpec((kdim, d), fixed),
                  pl.BlockSpec((tm, d), row), pl.BlockSpec((1, d), fixed), pl.BlockSpec((1, d), fixed),
                  pl.BlockSpec((d, 2 * _LANES), fixed), pl.BlockSpec((1, _LANES), fixed)],
        out_specs=[pl.BlockSpec((tm, d), row), pl.BlockSpec((tm, d), row), pl.BlockSpec((tm, _LANES), row),
                   pl.BlockSpec((tm, _LANES), row), pl.BlockSpec((tm, _LANES), row),
                   pl.BlockSpec((1, _SUBLANES, _LANES), lambda i: (i, 0, 0)),
                   pl.BlockSpec((_SUBLANES, _LANES), fixed)],
        out_shape=[jax.ShapeDtypeStruct((t, d), F32), jax.ShapeDtypeStruct((t, d), BF16),
                   jax.ShapeDtypeStruct((t, _LANES), I32), jax.ShapeDtypeStruct((t, _LANES), F32),
                   jax.ShapeDtypeStruct((t, _LANES), I32),
                   jax.ShapeDtypeStruct((t // tm, _SUBLANES, _LANES), I32),
                   jax.ShapeDtypeStruct((_SUBLANES, _LANES), I32)],
        scratch_shapes=[pltpu.VMEM((1, _LANES), F32)],
        compiler_params=_params("arbitrary"),
        name="post_mixer",
    )(o, w_out, x2, ln_g.reshape(1, d), ln_b.reshape(1, d), rw, rb)


def _run_rows(tile_tokens, n_exp):
    rows = tile_tokens * TOP_K + n_exp * (_RUN_CHUNK - 1)
    return -(-rows // _LANES) * _LANES


def _buffer_positions(eidx_ref, rank_ref, offs_ref):
    lane = lax.broadcasted_iota(I32, eidx_ref.shape, 1)
    offs = offs_ref[0, 0:1, :].astype(F32)
    pos = []
    for k in range(TOP_K):
        start = jnp.sum(jnp.where(lane == eidx_ref[:, k:k + 1], offs, 0.0), axis=1, keepdims=True)
        pos.append(start + rank_ref[:, k:k + 1].astype(F32))
    return pos


def _for_each_chunk(i, n_exp, src_ref, nch_ref, off_ref, fn):
    def expert(e, c):
        t = i * n_exp + e

        def chunk(j, c2):
            fn(src_ref[t] + j * _RUN_CHUNK, off_ref[t] + j * _RUN_CHUNK)
            return c2

        return lax.fori_loop(0, nch_ref[t], chunk, c)

    lax.fori_loop(0, n_exp, expert, 0)


def _chunk_slabs(ref, row):
    n = _RUN_CHUNK * _SUBLANES
    return ref.at[pl.ds(pl.multiple_of(row * _SUBLANES, _SUBLANES), n)]


def _dispatch_kernel(src_ref, nch_ref, off_ref, tot_ref, xb_ref, eidx_ref, rank_ref, offs_ref, xs_in_ref,
                     xs_ref, buf_ref, sem, *, n_exp):
    del xs_in_ref
    i = pl.program_id(0)
    tm = xb_ref.shape[0]
    n_buf = buf_ref.shape[0] // _SUBLANES
    lane = lax.broadcasted_iota(I32, eidx_ref.shape, 1)
    pos = jnp.full(eidx_ref.shape, -1.0, F32)
    for k, p in enumerate(_buffer_positions(eidx_ref, rank_ref, offs_ref)):
        pos = jnp.where(lane == k, p, pos)
    pos_t = pos.T
    buf_row = lax.broadcasted_iota(I32, (n_buf, tm), 0).astype(F32)
    place = buf_row == pos_t[0:1, :]
    for k in range(1, TOP_K):
        place = jnp.logical_or(place, buf_row == pos_t[k:k + 1, :])
    rows = jnp.dot(jnp.where(place, 1.0, 0.0).astype(BF16), xb_ref[...], preferred_element_type=F32)
    _store_slabs(buf_ref, rows)

    def copy(sorted_row, buffer_row):
        return pltpu.make_async_copy(_chunk_slabs(buf_ref, buffer_row), _chunk_slabs(xs_ref, sorted_row), sem)

    _for_each_chunk(i, n_exp, src_ref, nch_ref, off_ref, lambda s, b: copy(s, b).start())
    _wait_chunks(tot_ref[i], copy(0, 0))


def _wait_chunks(n, same_size_copy):
    def wait(j, c):
        same_size_copy.wait()
        return c

    lax.fori_loop(0, n, wait, 0)


def _dispatch(tables, x1b, eidx, rank, offs, n_rows):
    t, d = x1b.shape
    n_exp = N_EXPERTS
    tm = min(t, _MOE_TILE)
    xs0 = jnp.zeros((n_rows * _SUBLANES, _LANES), F32)
    row = lambda i, *_: (i, 0)
    return pl.pallas_call(
        functools.partial(_dispatch_kernel, n_exp=n_exp),
        grid_spec=pltpu.PrefetchScalarGridSpec(
            num_scalar_prefetch=4,
            grid=(t // tm,),
            in_specs=[pl.BlockSpec((tm, d), row), pl.BlockSpec((tm, _LANES), row), pl.BlockSpec((tm, _LANES), row),
                      pl.BlockSpec((1, _SUBLANES, _LANES), lambda i, *_: (i, 0, 0)),
                      pl.BlockSpec(memory_space=pl.ANY)],
            out_specs=pl.BlockSpec(memory_space=pl.ANY),
            scratch_shapes=[pltpu.VMEM((_run_rows(tm, n_exp) * _SUBLANES, _LANES), F32),
                            pltpu.SemaphoreType.DMA(())]),
        out_shape=jax.ShapeDtypeStruct(xs0.shape, xs0.dtype),
        input_output_aliases={8: 0},
        compiler_params=_params("arbitrary"),
        name="moe_dispatch",
    )(*tables, x1b, eidx, rank, offs, xs0)


def _ffn_kernel(first_blk_ref, n_blk_ref, xs_ref, w1_ref, b1_ref, w2_ref, b2_ref, ys_ref,
                w1b_ref, w2b_ref, xbuf_ref, ybuf_ref, in_sem, out_sem):
    e = pl.program_id(0)
    n_blk = n_blk_ref[e]
    block_rows = MOE_BLOCK * _SUBLANES

    def rows_of(ref, j):
        return ref.at[pl.ds(pl.multiple_of((first_blk_ref[e] + j) * block_rows, block_rows), block_rows)]

    def in_copy(j, slot):
        return pltpu.make_async_copy(rows_of(xs_ref, j), xbuf_ref.at[slot], in_sem.at[slot])

    def out_copy(j, slot):
        return pltpu.make_async_copy(ybuf_ref.at[slot], rows_of(ys_ref, j), out_sem.at[slot])

    @pl.when(n_blk > 0)
    def _():
        in_copy(0, 0).start()
        w1b_ref[...] = w1_ref[0, 0].astype(BF16)
        w2b_ref[...] = w2_ref[0, 0].astype(BF16)

    def block(j, c):
        slot = j % 2

        @pl.when(j + 1 < n_blk)
        def _():
            in_copy(j + 1, 1 - slot).start()

        in_copy(j, slot).wait()

        @pl.when(j >= 2)
        def _():
            out_copy(j - 2, slot).wait()

        f = w2b_ref.shape[0]
        x = _load_slabs(xbuf_ref.at[slot], MOE_BLOCK).astype(BF16)
        h = jnp.dot(x, w1b_ref[...], preferred_element_type=F32) + b1_ref[0, 0]
        glu = jnp.minimum(h[:, :f], SWIGLU_LIMIT)
        lin = jnp.clip(h[:, f:], -SWIGLU_LIMIT, SWIGLU_LIMIT)
        act = glu / (1.0 + jnp.exp(-SWIGLU_ALPHA * glu)) * (lin + 1.0)
        y = jnp.dot(act.astype(BF16), w2b_ref[...], preferred_element_type=F32) + b2_ref[0, 0]
        _store_slabs(ybuf_ref.at[slot], y)
        out_copy(j, slot).start()
        return c

    lax.fori_loop(0, n_blk, block, 0)

    @pl.when(n_blk >= 2)
    def _():
        out_copy(n_blk - 2, n_blk % 2).wait()

    @pl.when(n_blk >= 1)
    def _():
        out_copy(n_blk - 1, (n_blk - 1) % 2).wait()


def _expert_ffn(first_blk, n_blk, xs, layer, w1, b1, w2, b2):
    depth, n_exp, d, f2 = w1.shape
    f = w2.shape[2]
    by_expert = lambda e, fb, nb: (layer, e, 0, 0)
    block_rows = MOE_BLOCK * _SUBLANES
    return pl.pallas_call(
        _ffn_kernel,
        grid_spec=pltpu.PrefetchScalarGridSpec(
            num_scalar_prefetch=2,
            grid=(n_exp,),
            in_specs=[pl.BlockSpec(memory_space=pl.ANY),
                      pl.BlockSpec((1, 1, d, f2), by_expert), pl.BlockSpec((1, 1, 1, f2), by_expert),
                      pl.BlockSpec((1, 1, f, d), by_expert), pl.BlockSpec((1, 1, 1, d), by_expert)],
            out_specs=pl.BlockSpec(memory_space=pl.ANY),
            scratch_shapes=[pltpu.VMEM((d, f2), BF16), pltpu.VMEM((f, d), BF16),
                            pltpu.VMEM((2, block_rows, _LANES), F32), pltpu.VMEM((2, block_rows, _LANES), F32),
                            pltpu.SemaphoreType.DMA((2,)), pltpu.SemaphoreType.DMA((2,))]),
        out_shape=jax.ShapeDtypeStruct(xs.shape, F32),
        input_output_aliases={2: 0},
        compiler_params=_params("arbitrary"),
        name="moe_expert_ffn",
    )(first_blk, n_blk, xs, w1, b1.reshape(depth, n_exp, 1, f2), w2, b2.reshape(depth, n_exp, 1, d))


def _combine_kernel(src_ref, nch_ref, off_ref, tot_ref, ys_ref, gate_ref, eidx_ref, rank_ref, offs_ref, x1_ref,
                    g_ref, b_ref, x2_ref, x2b_ref, buf_ref, sem, *, n_exp):
    i = pl.program_id(0)
    tm = x1_ref.shape[0]
    n_buf = buf_ref.shape[0] // _SUBLANES

    @pl.when(i == 0)
    def _():
        buf_ref[...] = jnp.zeros_like(buf_ref)

    def copy(sorted_row, buffer_row):
        return pltpu.make_async_copy(_chunk_slabs(ys_ref, sorted_row), _chunk_slabs(buf_ref, buffer_row), sem)

    _for_each_chunk(i, n_exp, src_ref, nch_ref, off_ref, lambda s, b: copy(s, b).start())
    buf_col = lax.broadcasted_iota(I32, (tm, n_buf), 1).astype(F32)
    weights = jnp.zeros((tm, n_buf), F32)
    for k, p in enumerate(_buffer_positions(eidx_ref, rank_ref, offs_ref)):
        weights = jnp.where(buf_col == p, gate_ref[:, k:k + 1], weights)
    _wait_chunks(tot_ref[i], copy(0, 0))
    y = jnp.dot(weights.astype(BF16), _load_slabs(buf_ref, n_buf).astype(BF16), preferred_element_type=F32)
    x2 = _layer_norm(DEEPNORM_ALPHA * x1_ref[...] + y, g_ref[...], b_ref[...])
    x2_ref[...] = x2
    x2b_ref[...] = x2.astype(BF16)


def _combine(tables, ys, gates, eidx, rank, offs, x1, ln_g, ln_b):
    t, d = x1.shape
    n_exp = N_EXPERTS
    tm = min(t, _MOE_TILE)
    row = lambda i, *_: (i, 0)
    fixed = lambda i, *_: (0, 0)
    return pl.pallas_call(
        functools.partial(_combine_kernel, n_exp=n_exp),
        grid_spec=pltpu.PrefetchScalarGridSpec(
            num_scalar_prefetch=4,
            grid=(t // tm,),
            in_specs=[pl.BlockSpec(memory_space=pl.ANY), pl.BlockSpec((tm, _LANES), row),
                      pl.BlockSpec((tm, _LANES), row), pl.BlockSpec((tm, _LANES), row),
                      pl.BlockSpec((1, _SUBLANES, _LANES), lambda i, *_: (i, 0, 0)),
                      pl.BlockSpec((tm, d), row), pl.BlockSpec((1, d), fixed), pl.BlockSpec((1, d), fixed)],
            out_specs=[pl.BlockSpec((tm, d), row), pl.BlockSpec((tm, d), row)],
            scratch_shapes=[pltpu.VMEM((_run_rows(tm, n_exp) * _SUBLANES, _LANES), F32),
                            pltpu.SemaphoreType.DMA(())]),
        out_shape=[jax.ShapeDtypeStruct((t, d), F32), jax.ShapeDtypeStruct((t, d), BF16)],
        compiler_params=_params("arbitrary"),
        name="moe_combine",
    )(*tables, ys, gates, eidx, rank, offs, x1, ln_g.reshape(1, d), ln_b.reshape(1, d))


def _moe_layer(x1, x1b, eidx, gates, rank, starts, counts, layer, w1, b1, w2, b2, ln_g, ln_b):
    t = x1.shape[0]
    n_exp = w1.shape[1]
    slack = _RUN_CHUNK - 1
    n_blocks = -(-(t * TOP_K + n_exp * slack) // MOE_BLOCK) + n_exp
    cnt = counts[0, :n_exp]
    padded = ((cnt + slack + MOE_BLOCK - 1) // MOE_BLOCK) * MOE_BLOCK
    pad_starts = jnp.cumsum(padded) - padded
    run_start = starts[:, 0, :n_exp]
    run_len = jnp.concatenate([run_start[1:], cnt[None]], axis=0) - run_start
    n_chunks = (run_len + slack) // _RUN_CHUNK
    buf_off = (jnp.cumsum(n_chunks, axis=1) - n_chunks) * _RUN_CHUNK
    tables = ((pad_starts[None, :] + run_start).astype(I32).reshape(-1), n_chunks.astype(I32).reshape(-1),
              buf_off.astype(I32).reshape(-1), jnp.sum(n_chunks, axis=1).astype(I32))
    offs = jnp.zeros(starts.shape, I32).at[:, :, :n_exp].set(buf_off[:, None, :].astype(I32))
    xs = _dispatch(tables, x1b, eidx, rank, offs, n_blocks * MOE_BLOCK)
    ys = _expert_ffn((pad_starts // MOE_BLOCK).astype(I32), (padded // MOE_BLOCK).astype(I32), xs,
                     layer, w1, b1, w2, b2)
    return _combine(tables, ys, gates, eidx, rank, offs, x1, ln_g, ln_b)


def _fox_mixer(x2, xb, w_in, b_f, batch, seq):
    d = x2.shape[1]
    qkv = _matmul(xb, w_in[:, :3 * d].astype(BF16), BF16)
    c_pad = _fox_gate_cumsum(x2, w_in[:, 3 * d:], b_f, batch, seq)
    return _fox_attention(qkv, c_pad, batch, seq, FOX_HEADS)


def _moba_mixer(xb, w_in, batch, seq):
    return _moba_attention(_matmul(xb, w_in.astype(BF16), BF16), batch, seq, MOBA_HEADS)


def _retention_mixer(xb, w_in, gn_g, batch, seq):
    return _retention(_matmul(xb, w_in.astype(BF16), BF16), gn_g, batch, seq, RET_HEADS)


def kernel(x, fox_w_in, fox_b_f, fox_w_out, moba_w_in, moba_w_out, ret_w_in, ret_gn_g, ret_w_out, ln_g, ln_b,
           router_w, router_b, moe_w1, moe_b1, moe_w2, moe_b2):
    batch, seq, d = x.shape
    x2 = x.reshape(batch * seq, d)
    xb = x2.astype(BF16)
    for i in range(DEPTH):
        kind, j = i % 3, i // 3
        if kind == 0:
            o, w_out = _fox_mixer(x2, xb, fox_w_in[j], fox_b_f[j], batch, seq), fox_w_out[j]
        elif kind == 1:
            o, w_out = _moba_mixer(xb, moba_w_in[j], batch, seq), moba_w_out[j]
        else:
            o, w_out = _retention_mixer(xb, ret_w_in[j], ret_gn_g[j], batch, seq), ret_w_out[j]
        routed = _post_mixer(o, w_out.astype(BF16), x2, ln_g[i, 0], ln_b[i, 0], router_w[i], router_b[i])
        x2, xb = _moe_layer(*routed, i, moe_w1, moe_b1, moe_w2, moe_b2, ln_g[i, 1], ln_b[i, 1])
    return x2.reshape(batch, seq, d)
```

```python
import functools

import jax
import jax.numpy as jnp
from jax import lax
from jax.experimental import pallas as pl
from jax.experimental.pallas import tpu as pltpu

F32, BF16, I32 = jnp.float32, jnp.bfloat16, jnp.int32
_HIGHEST = lax.Precision.HIGHEST
_NT = (((1,), (1,)), ((), ()))
_LANES = 128
_SUBLANES = 8
_MASKED = -1e30
_BIAS_OFF = -32768.0
_LOG2E = 1.4426950408889634
_VMEM_LIMIT = 56 * 1024 * 1024
_MOE_TILE = 512
_RUN_CHUNK = 16
_ROW_DMA_PRIORITY = 1

DEPTH = 4
FOX_HEADS, MOBA_HEADS, RET_HEADS = 8, 8, 4
MOBA_BLOCK, MOBA_TOPK = 256, 3
RET_ROPE_BASE, RET_GN_EPS = 10000.0, 1e-6
N_EXPERTS, TOP_K, MOE_BLOCK = 32, 4, 256
SWIGLU_LIMIT, SWIGLU_ALPHA = 7.0, 1.702
LN_EPS = 1e-5
DEEPNORM_ALPHA = (2 * DEPTH) ** 0.25


def _params(*sem):
    return pltpu.CompilerParams(dimension_semantics=sem, vmem_limit_bytes=_VMEM_LIMIT)


def _layer_norm(y, g, b):
    mu = jnp.mean(y, axis=-1, keepdims=True)
    d = y - mu
    var = jnp.mean(d * d, axis=-1, keepdims=True)
    return d * lax.rsqrt(var + LN_EPS) * g + b


def _store_slabs(ref, val):
    n = val.shape[0]
    for s in range(_SUBLANES):
        ref[pl.ds(s, n, stride=_SUBLANES), :] = val[:, s * _LANES:(s + 1) * _LANES]


def _load_slabs(ref, n):
    return jnp.concatenate([ref[pl.ds(s, n, stride=_SUBLANES), :] for s in range(_SUBLANES)], axis=1)


def _mm_kernel(a_ref, w_ref, o_ref):
    o_ref[...] = jnp.dot(a_ref[...], w_ref[...], preferred_element_type=F32).astype(o_ref.dtype)


def _matmul(a, w, out_dtype):
    m, k = a.shape
    n = w.shape[1]
    tm, tn = min(m, 1024), min(n, 1024)
    return pl.pallas_call(
        _mm_kernel,
        grid=(m // tm, n // tn),
        in_specs=[pl.BlockSpec((tm, k), lambda i, j: (i, 0)),
                  pl.BlockSpec((k, tn), lambda i, j: (0, j))],
        out_specs=pl.BlockSpec((tm, tn), lambda i, j: (i, j)),
        out_shape=jax.ShapeDtypeStruct((m, n), out_dtype),
        compiler_params=_params("parallel", "parallel"),
        name="proj_matmul",
    )(a, w)


def _flash_first(s, v):
    m = jnp.max(s, axis=1, keepdims=True)
    p = jnp.exp2(s - m)
    l = jnp.sum(p, axis=1, keepdims=True)
    acc = jnp.dot(p.astype(BF16), v, preferred_element_type=F32)
    return m, l, acc


def _flash_next(carry, s, v):
    m, l, acc = carry
    m_new = jnp.maximum(m, jnp.max(s, axis=1, keepdims=True))
    a = jnp.exp2(m - m_new)
    p = jnp.exp2(s - m_new)
    l = a * l + jnp.sum(p, axis=1, keepdims=True)
    acc = a * acc + jnp.dot(p.astype(BF16), v, preferred_element_type=F32)
    return m_new, l, acc


def _causal_flash(q_aug, kaug_ref, v_ref, i, tq):
    def scores(j, width):
        start = pl.multiple_of(j * tq, tq)
        return lax.dot_general(q_aug, kaug_ref[pl.ds(start, width), :], _NT, preferred_element_type=F32)

    def values(j, width):
        return v_ref[pl.ds(pl.multiple_of(j * tq, tq), width), :]

    row = lax.broadcasted_iota(I32, (tq, tq), 0)
    col = lax.broadcasted_iota(I32, (tq, tq), 1)
    carry = _flash_first(jnp.where(col <= row, scores(i, tq), _MASKED), values(i, tq))
    carry = lax.fori_loop(
        0, i // 2, lambda p, c: _flash_next(c, scores(2 * p, 2 * tq), values(2 * p, 2 * tq)), carry)
    _, l, acc = lax.cond(i % 2 == 1, lambda c: _flash_next(c, scores(i - 1, tq), values(i - 1, tq)),
                         lambda c: c, carry)
    return acc / l


def _lane_columns(n, cols):
    lane = lax.broadcasted_iota(I32, (n, _LANES), 1)
    out = jnp.zeros((n, _LANES), F32)
    for t, c in enumerate(cols):
        out = jnp.where(lane == t, c, out)
    return out.astype(BF16)


def _split3(c):
    hi = c.astype(BF16).astype(F32)
    r = c - hi
    mid = r.astype(BF16).astype(F32)
    lo = (r - mid).astype(BF16).astype(F32)
    return [hi, mid, lo]


def _fox_gate_kernel(x_ref, wf_ref, bf_ref, c_ref, carry_ref):
    @pl.when(pl.program_id(1) == 0)
    def _():
        carry_ref[...] = jnp.zeros_like(carry_ref)

    x = x_ref[...]
    x_hi = x.astype(BF16)
    x_lo = (x - x_hi.astype(F32)).astype(BF16)
    hi_part = jnp.dot(x_hi, wf_ref[...], preferred_element_type=F32)
    lo_part = jnp.dot(x_lo, wf_ref[:, :_LANES], preferred_element_type=F32)
    z = hi_part[:, :_LANES] + hi_part[:, _LANES:] + lo_part + bf_ref[...]
    log_f = jnp.minimum(z, 0.0) - jnp.log1p(jnp.exp(-jnp.abs(z)))
    tc = z.shape[0]
    row = lax.broadcasted_iota(I32, (tc, tc), 0)
    col = lax.broadcasted_iota(I32, (tc, tc), 1)
    tri = jnp.where(col <= row, 1.0, 0.0).astype(BF16)
    hi, mid, lo = [p.astype(BF16) for p in _split3(log_f)]
    hi_mid = jnp.dot(tri, jnp.concatenate([hi, mid], axis=1), preferred_element_type=F32)
    c = (hi_mid[:, :_LANES] + hi_mid[:, _LANES:] + jnp.dot(tri, lo, preferred_element_type=F32)
         + carry_ref[...])
    c_ref[...] = c
    carry_ref[...] = c[tc - 1:tc, :]


def _fox_gate_cumsum(x2, w_f, b_f, batch, seq):
    t, d = x2.shape
    h = w_f.shape[1]
    wf = jnp.zeros((d, _LANES), F32).at[:, :h].set(w_f)
    wf_hi = wf.astype(BF16)
    wf = jnp.concatenate([wf_hi, (wf - wf_hi.astype(F32)).astype(BF16)], axis=1)
    bf = jnp.zeros((1, _LANES), F32).at[0, :h].set(b_f)
    tc = min(seq, 512)
    ns = seq // tc
    return pl.pallas_call(
        _fox_gate_kernel,
        grid=(batch, ns),
        in_specs=[pl.BlockSpec((tc, d), lambda b, s: (b * ns + s, 0)),
                  pl.BlockSpec((d, 2 * _LANES), lambda b, s: (0, 0)),
                  pl.BlockSpec((1, _LANES), lambda b, s: (0, 0))],
        out_specs=pl.BlockSpec((tc, _LANES), lambda b, s: (b * ns + s, 0)),
        out_shape=jax.ShapeDtypeStruct((t, _LANES), F32),
        scratch_shapes=[pltpu.VMEM((1, _LANES), F32)],
        compiler_params=_params("parallel", "arbitrary"),
        name="fox_gate_cumsum",
    )(x2, wf, bf)


def _fox_attn_kernel(q_ref, k_ref, v_ref, cq_ref, ck_ref, o_ref, kaug_ref, *, scale):
    h = pl.program_id(1)
    i = pl.program_id(2)
    tq, hd = q_ref.shape
    seq = k_ref.shape[0]

    def head_column(c_ref, rows):
        lane = lax.broadcasted_iota(I32, (rows, _LANES), 1)
        return jnp.sum(jnp.where(lane == h, c_ref[...], 0.0), axis=1, keepdims=True) * _LOG2E

    @pl.when(i == 0)
    def _():
        kaug_ref[:, :hd] = k_ref[...]
        kaug_ref[:, hd:] = _lane_columns(seq, [1.0, 1.0, 1.0] + [-c for c in _split3(head_column(ck_ref, seq))])

    q = (q_ref[...].astype(F32) * (scale * _LOG2E)).astype(BF16)
    q_aug = jnp.concatenate([q, _lane_columns(tq, _split3(head_column(cq_ref, tq)) + [1.0, 1.0, 1.0])], axis=1)
    o_ref[...] = _causal_flash(q_aug, kaug_ref, v_ref, i, tq).astype(o_ref.dtype)


def _fox_attention(qkv, c_pad, batch, seq, heads):
    t = qkv.shape[0]
    d = qkv.shape[1] // 3
    hd = d // heads
    tq = min(seq, 512)
    nq = seq // tq
    return pl.pallas_call(
        functools.partial(_fox_attn_kernel, scale=hd ** -0.5),
        grid=(batch, heads, nq),
        in_specs=[pl.BlockSpec((tq, hd), lambda b, h, i: (b * nq + i, h)),
                  pl.BlockSpec((seq, hd), lambda b, h, i: (b, heads + h)),
                  pl.BlockSpec((seq, hd), lambda b, h, i: (b, 2 * heads + h)),
                  pl.BlockSpec((tq, _LANES), lambda b, h, i: (b * nq + i, 0)),
                  pl.BlockSpec((seq, _LANES), lambda b, h, i: (b, 0))],
        out_specs=pl.BlockSpec((tq, hd), lambda b, h, i: (b * nq + i, h)),
        out_shape=jax.ShapeDtypeStruct((t, d), BF16),
        scratch_shapes=[pltpu.VMEM((seq, hd + _LANES), BF16)],
        compiler_params=_params("parallel", "parallel", "arbitrary"),
        name="fox_attention",
    )(qkv, qkv, qkv, c_pad, c_pad)


def _moba_attn_kernel(q_ref, k_ref, v_ref, o_ref, kaug_ref, kmean_ref, *, scale, blk, topk):
    i = pl.program_id(2)
    tq, hd = q_ref.shape
    seq = k_ref.shape[0]
    nb = seq // blk

    @pl.when(i == 0)
    def _():
        kmean_ref[...] = jnp.zeros_like(kmean_ref)
        for n in range(nb):
            kb = k_ref[n * blk:(n + 1) * blk, :].astype(F32)
            kmean_ref[n:n + 1, :] = jnp.mean(kb, axis=0, keepdims=True)
        kaug_ref[:, :hd] = k_ref[...]
        key_block = lax.broadcasted_iota(I32, (seq, _LANES), 0) // blk
        lane = lax.broadcasted_iota(I32, (seq, _LANES), 1)
        kaug_ref[:, hd:] = jnp.where(key_block == lane, 1.0, 0.0).astype(BF16)

    q = q_ref[...]
    gate = lax.dot_general(q.astype(F32), kmean_ref[...], _NT, precision=_HIGHEST,
                           preferred_element_type=F32)
    lane = lax.broadcasted_iota(I32, gate.shape, 1)
    lane_f = lane.astype(F32)
    own = (i * tq + lax.broadcasted_iota(I32, gate.shape, 0)) // blk
    gate = jnp.where(lane < own, gate, -jnp.inf)
    allowed = lane == own
    for r in range(topk):
        mx = jnp.max(gate, axis=1, keepdims=True)
        idx = jnp.min(jnp.where(gate == mx, lane_f, float(_LANES)), axis=1, keepdims=True)
        pick = lane_f == idx
        allowed = jnp.logical_or(allowed, jnp.logical_and(pick, r < own))
        gate = jnp.where(pick, -jnp.inf, gate)
    bias = jnp.where(allowed, 0.0, _BIAS_OFF).astype(BF16)
    q_aug = jnp.concatenate([(q.astype(F32) * (scale * _LOG2E)).astype(BF16), bias], axis=1)
    o_ref[...] = _causal_flash(q_aug, kaug_ref, v_ref, i, tq).astype(o_ref.dtype)


def _moba_attention(qkv, batch, seq, heads):
    t = qkv.shape[0]
    d = qkv.shape[1] // 3
    hd = d // heads
    blk = MOBA_BLOCK
    assert seq % blk == 0 and seq // blk <= _LANES
    tq = 2 * blk if seq % (2 * blk) == 0 else blk
    nq = seq // tq
    return pl.pallas_call(
        functools.partial(_moba_attn_kernel, scale=hd ** -0.5, blk=blk, topk=min(MOBA_TOPK, seq // blk)),
        grid=(batch, heads, nq),
        in_specs=[pl.BlockSpec((tq, hd), lambda b, h, i: (b * nq + i, h)),
                  pl.BlockSpec((seq, hd), lambda b, h, i: (b, heads + h)),
                  pl.BlockSpec((seq, hd), lambda b, h, i: (b, 2 * heads + h))],
        out_specs=pl.BlockSpec((tq, hd), lambda b, h, i: (b * nq + i, h)),
        out_shape=jax.ShapeDtypeStruct((t, d), BF16),
        scratch_shapes=[pltpu.VMEM((seq, hd + _LANES), BF16), pltpu.VMEM((_LANES, hd), F32)],
        compiler_params=_params("parallel", "parallel", "arbitrary"),
        name="moba_attention",
    )(qkv, qkv, qkv)


def _retention_kernel(q_ref, k_ref, v_ref, g_ref, cos_ref, sin_ref, dm_ref, xi_ref, zeta_ref, gc_ref,
                      gn_ref, o_ref, state_ref, *, k_scale):
    @pl.when(pl.program_id(2) == 0)
    def _():
        state_ref[...] = jnp.zeros_like(state_ref)

    cos, sin = cos_ref[...], sin_ref[...]
    half = cos.shape[1]

    def rotate(t):
        t = t.astype(F32)
        t1, t2 = t[:, :half], t[:, half:]
        return jnp.concatenate([t1 * cos - t2 * sin, t1 * sin + t2 * cos], axis=1)

    q = rotate(q_ref[...]).astype(BF16)
    k_rot = rotate(k_ref[...]) * k_scale
    k = k_rot.astype(BF16)
    k_t = k_rot.T.astype(BF16)
    v = v_ref[...]
    s = lax.dot_general(q, k, _NT, preferred_element_type=F32) * dm_ref[0]
    inner = jnp.dot(s.astype(BF16), v, preferred_element_type=F32)
    state = state_ref[...]
    cross = jnp.dot(q, state.astype(BF16), preferred_element_type=F32) * xi_ref[0]
    o = inner + cross
    vz = (v.astype(F32) * zeta_ref[0]).astype(BF16)
    state_ref[...] = gc_ref[0][:, :1] * state + jnp.dot(k_t, vz, preferred_element_type=F32)

    mu = jnp.mean(o, axis=-1, keepdims=True)
    d = o - mu
    var = jnp.mean(d * d, axis=-1, keepdims=True)
    o = d * lax.rsqrt(var + RET_GN_EPS) * gn_ref[...]
    g = g_ref[...].astype(F32)
    o_ref[...] = (g / (1.0 + jnp.exp(-g)) * o).astype(o_ref.dtype)


def _retention(proj, gn_g, batch, seq, heads):
    t, width = proj.shape
    dk = width // (6 * heads)
    dv = 2 * dk
    chunk = min(seq, 256)
    nc = seq // chunk
    pos = jnp.arange(seq, dtype=F32)
    inv_freq = jnp.exp(-jnp.log(RET_ROPE_BASE) * jnp.arange(0, dk, 2, dtype=F32) / dk)
    ang = pos[:, None] * inv_freq[None, :]
    cos, sin = jnp.cos(ang), jnp.sin(ang)
    log_gamma = jnp.log(1.0 - jnp.exp2(-5.0 - jnp.arange(heads, dtype=F32)))
    n = jnp.arange(chunk, dtype=F32)
    diff = n[:, None] - n[None, :]
    d_mask = jnp.where(diff[None] >= 0, jnp.exp(diff[None] * log_gamma[:, None, None]), 0.0)
    xi = jnp.exp((n[None, :] + 1.0) * log_gamma[:, None])[:, :, None]
    zeta = jnp.exp((chunk - 1.0 - n[None, :]) * log_gamma[:, None])[:, :, None]
    g_chunk = jnp.broadcast_to(jnp.exp(chunk * log_gamma)[:, None, None], (heads, 1, _LANES))
    return pl.pallas_call(
        functools.partial(_retention_kernel, k_scale=dk ** -0.5),
        grid=(batch, heads, nc),
        in_specs=[pl.BlockSpec((chunk, dk), lambda b, h, c: (b * nc + c, h)),
                  pl.BlockSpec((chunk, dk), lambda b, h, c: (b * nc + c, heads + h)),
                  pl.BlockSpec((chunk, dv), lambda b, h, c: (b * nc + c, heads + h)),
                  pl.BlockSpec((chunk, dv), lambda b, h, c: (b * nc + c, 2 * heads + h)),
                  pl.BlockSpec((chunk, dk // 2), lambda b, h, c: (c, 0)),
                  pl.BlockSpec((chunk, dk // 2), lambda b, h, c: (c, 0)),
                  pl.BlockSpec((1, chunk, chunk), lambda b, h, c: (h, 0, 0)),
                  pl.BlockSpec((1, chunk, 1), lambda b, h, c: (h, 0, 0)),
                  pl.BlockSpec((1, chunk, 1), lambda b, h, c: (h, 0, 0)),
                  pl.BlockSpec((1, 1, _LANES), lambda b, h, c: (h, 0, 0)),
                  pl.BlockSpec((1, dv), lambda b, h, c: (0, h))],
        out_specs=pl.BlockSpec((chunk, dv), lambda b, h, c: (b * nc + c, h)),
        out_shape=jax.ShapeDtypeStruct((t, heads * dv), BF16),
        scratch_shapes=[pltpu.VMEM((dk, dv), F32)],
        compiler_params=_params("parallel", "parallel", "arbitrary"),
        name="retention",
    )(proj, proj, proj, proj, cos, sin, d_mask, xi, zeta, g_chunk, gn_g.reshape(1, -1))


def _post_mixer_kernel(o_ref, w_ref, x_ref, g_ref, b_ref, rw_ref, rb_ref,
                       x1_ref, x1b_ref, eidx_ref, gate_ref, rank_ref, start_ref, cnt_ref, run_ref, *, n_exp, topk):
    @pl.when(pl.program_id(0) == 0)
    def _():
        run_ref[...] = jnp.zeros_like(run_ref)

    h = jnp.dot(o_ref[...], w_ref[...], preferred_element_type=F32)
    x1 = _layer_norm(DEEPNORM_ALPHA * x_ref[...] + h, g_ref[...], b_ref[...])
    x1_ref[...] = x1
    x1b_ref[...] = x1.astype(BF16)
    start_ref[...] = jnp.broadcast_to(run_ref[...], start_ref.shape[1:]).astype(I32)[None]

    x_hi = x1.astype(BF16)
    x_lo = (x1 - x_hi.astype(F32)).astype(BF16)
    hi_part = jnp.dot(x_hi, rw_ref[...], preferred_element_type=F32)
    lo_part = jnp.dot(x_lo, rw_ref[:, :_LANES], preferred_element_type=F32)
    logits = hi_part[:, :_LANES] + hi_part[:, _LANES:] + lo_part + rb_ref[...]
    tm = logits.shape[0]
    lane = lax.broadcasted_iota(I32, logits.shape, 1).astype(F32)
    logits = jnp.where(lane < n_exp, logits, -jnp.inf)
    vals, idxs = [], []
    for _ in range(topk):
        mx = jnp.max(logits, axis=1, keepdims=True)
        idx = jnp.min(jnp.where(logits == mx, lane, float(_LANES)), axis=1, keepdims=True)
        vals.append(mx)
        idxs.append(idx)
        logits = jnp.where(lane == idx, -jnp.inf, logits)
    exps = [jnp.exp(v - vals[0]) for v in vals]
    den = exps[0]
    for e in exps[1:]:
        den = den + e

    picked = jnp.zeros(logits.shape, F32)
    for idx in idxs:
        picked = jnp.where(lane == idx, 1.0, picked)
    row = lax.broadcasted_iota(I32, (tm, tm), 0)
    col = lax.broadcasted_iota(I32, (tm, tm), 1)
    before = (col < row).astype(BF16)
    prefix = jnp.dot(before, picked.astype(BF16), preferred_element_type=F32)

    eidx = jnp.zeros(logits.shape, F32)
    gates = jnp.zeros(logits.shape, F32)
    ranks = jnp.zeros(logits.shape, F32)
    for r in range(topk):
        rank_r = jnp.sum(jnp.where(lane == idxs[r], prefix, 0.0), axis=1, keepdims=True)
        eidx = jnp.where(lane == r, idxs[r], eidx)
        gates = jnp.where(lane == r, exps[r] / den, gates)
        ranks = jnp.where(lane == r, rank_r, ranks)
    eidx_ref[...] = eidx.astype(I32)
    gate_ref[...] = gates
    rank_ref[...] = ranks.astype(I32)
    run = run_ref[...] + jnp.sum(picked, axis=0, keepdims=True)
    run_ref[...] = run
    cnt_ref[...] = jnp.broadcast_to(run, cnt_ref.shape).astype(I32)


def _post_mixer(o, w_out, x2, ln_g, ln_b, router_w, router_b):
    t, d = x2.shape
    assert d == _SUBLANES * _LANES
    kdim = o.shape[1]
    n_exp = router_w.shape[1]
    tm = min(t, _MOE_TILE)
    rw = jnp.zeros((d, _LANES), F32).at[:, :n_exp].set(router_w)
    rw_hi = rw.astype(BF16)
    rw = jnp.concatenate([rw_hi, (rw - rw_hi.astype(F32)).astype(BF16)], axis=1)
    rb = jnp.zeros((1, _LANES), F32).at[0, :n_exp].set(router_b)
    row = lambda i: (i, 0)
    fixed = lambda i: (0, 0)
    return pl.pallas_call(
        functools.partial(_post_mixer_kernel, n_exp=n_exp, topk=TOP_K),
        grid=(t // tm,),
        in_specs=[pl.BlockSpec((tm, kdim), row), pl.BlockSpec((kdim, d), fixed),
                  pl.BlockSpec((tm, d), row), pl.BlockSpec((1, d), fixed), pl.BlockSpec((1, d), fixed),
                  pl.BlockSpec((d, 2 * _LANES), fixed), pl.BlockSpec((1, _LANES), fixed)],
        out_specs=[pl.BlockSpec((tm, d), row), pl.BlockSpec((tm, d), row), pl.BlockSpec((tm, _LANES), row),
                   pl.BlockSpec((tm, _LANES), row), pl.BlockSpec((tm, _LANES), row),
                   pl.BlockSpec((1, _SUBLANES, _LANES), lambda i: (i, 0, 0)),
                   pl.BlockSpec((_SUBLANES, _LANES), fixed)],
        out_shape=[jax.ShapeDtypeStruct((t, d), F32), jax.ShapeDtypeStruct((t, d), BF16),
                   jax.ShapeDtypeStruct((t, _LANES), I32), jax.ShapeDtypeStruct((t, _LANES), F32),
                   jax.ShapeDtypeStruct((t, _LANES), I32),
                   jax.ShapeDtypeStruct((t // tm, _SUBLANES, _LANES), I32),
                   jax.ShapeDtypeStruct((_SUBLANES, _LANES), I32)],
        scratch_shapes=[pltpu.VMEM((1, _LANES), F32)],
        compiler_params=_params("arbitrary"),
        name="post_mixer",
    )(o, w_out, x2, ln_g.reshape(1, d), ln_b.reshape(1, d), rw, rb)


def _run_rows(tile_tokens, n_exp):
    rows = tile_tokens * TOP_K + n_exp * (_RUN_CHUNK - 1)
    return -(-rows // _LANES) * _LANES


def _buffer_positions(eidx_ref, rank_ref, offs_ref):
    lane = lax.broadcasted_iota(I32, eidx_ref.shape, 1)
    offs = offs_ref[0, 0:1, :].astype(F32)
    pos = []
    for k in range(TOP_K):
        start = jnp.sum(jnp.where(lane == eidx_ref[:, k:k + 1], offs, 0.0), axis=1, keepdims=True)
        pos.append(start + rank_ref[:, k:k + 1].astype(F32))
    return pos


def _for_each_chunk(i, n_exp, src_ref, nch_ref, off_ref, fn):
    def expert(e, c):
        t = i * n_exp + e

        def chunk(j, c2):
            fn(src_ref[t] + j * _RUN_CHUNK, off_ref[t] + j * _RUN_CHUNK)
            return c2

        return lax.fori_loop(0, nch_ref[t], chunk, c)

    lax.fori_loop(0, n_exp, expert, 0)


def _chunk_slabs(ref, row):
    n = _RUN_CHUNK * _SUBLANES
    return ref.at[pl.ds(pl.multiple_of(row * _SUBLANES, _SUBLANES), n)]


def _dispatch_kernel(src_ref, nch_ref, off_ref, tot_ref, zsrc_ref, zn_ref, ztot_ref, xb_ref, eidx_ref, rank_ref,
                     offs_ref, xs_ref, buf_ref, zero_ref, sem, *, n_exp):
    i = pl.program_id(0)
    tm = xb_ref.shape[0]
    n_buf = buf_ref.shape[0] // _SUBLANES

    @pl.when(i == 0)
    def _():
        zero_ref[...] = jnp.zeros_like(zero_ref)

        def zero_copy(sorted_row):
            return pltpu.make_async_copy(zero_ref, _chunk_slabs(xs_ref, sorted_row), sem)

        def region(e, c):
            def chunk(j, c2):
                zero_copy(zsrc_ref[e] + j * _RUN_CHUNK).start()
                return c2

            return lax.fori_loop(0, zn_ref[e], chunk, c)

        lax.fori_loop(0, n_exp + 1, region, 0)
        _wait_chunks(ztot_ref[0], zero_copy(0))

    lane = lax.broadcasted_iota(I32, eidx_ref.shape, 1)
    pos = jnp.full(eidx_ref.shape, -1.0, F32)
    for k, p in enumerate(_buffer_positions(eidx_ref, rank_ref, offs_ref)):
        pos = jnp.where(lane == k, p, pos)
    pos_t = pos.T
    buf_row = lax.broadcasted_iota(I32, (n_buf, tm), 0).astype(F32)
    place = buf_row == pos_t[0:1, :]
    for k in range(1, TOP_K):
        place = jnp.logical_or(place, buf_row == pos_t[k:k + 1, :])
    rows = jnp.dot(jnp.where(place, 1.0, 0.0).astype(BF16), xb_ref[...], preferred_element_type=F32)
    _store_slabs(buf_ref, rows)

    def copy(sorted_row, buffer_row):
        return pltpu.make_async_copy(_chunk_slabs(buf_ref, buffer_row), _chunk_slabs(xs_ref, sorted_row), sem)

    _for_each_chunk(i, n_exp, src_ref, nch_ref, off_ref, lambda s, b: copy(s, b).start())
    _wait_chunks(tot_ref[i], copy(0, 0))


def _wait_chunks(n, same_size_copy):
    def wait(j, c):
        same_size_copy.wait()
        return c

    lax.fori_loop(0, n, wait, 0)


def _dispatch(tables, zero_tables, x1b, eidx, rank, offs, n_rows):
    t, d = x1b.shape
    n_exp = N_EXPERTS
    tm = min(t, _MOE_TILE)
    row = lambda i, *_: (i, 0)
    return pl.pallas_call(
        functools.partial(_dispatch_kernel, n_exp=n_exp),
        grid_spec=pltpu.PrefetchScalarGridSpec(
            num_scalar_prefetch=7,
            grid=(t // tm,),
            in_specs=[pl.BlockSpec((tm, d), row), pl.BlockSpec((tm, _LANES), row), pl.BlockSpec((tm, _LANES), row),
                      pl.BlockSpec((1, _SUBLANES, _LANES), lambda i, *_: (i, 0, 0))],
            out_specs=pl.BlockSpec(memory_space=pl.ANY),
            scratch_shapes=[pltpu.VMEM((_run_rows(tm, n_exp) * _SUBLANES, _LANES), F32),
                            pltpu.VMEM((_RUN_CHUNK * _SUBLANES, _LANES), F32),
                            pltpu.SemaphoreType.DMA(())]),
        out_shape=jax.ShapeDtypeStruct((n_rows * _SUBLANES, _LANES), F32),
        compiler_params=_params("arbitrary"),
        name="moe_dispatch",
    )(*tables, *zero_tables, x1b, eidx, rank, offs)


def _ffn_kernel(first_blk_ref, n_blk_ref, xs_ref, w1_ref, b1_ref, w2_ref, b2_ref, ys_ref,
                w1b_ref, w2b_ref, xbuf_ref, ybuf_ref, in_sem, out_sem):
    e = pl.program_id(0)
    n_blk = n_blk_ref[e]
    block_rows = MOE_BLOCK * _SUBLANES

    def rows_of(ref, j):
        return ref.at[pl.ds(pl.multiple_of((first_blk_ref[e] + j) * block_rows, block_rows), block_rows)]

    def in_copy(j, slot):
        return pltpu.make_async_copy(rows_of(xs_ref, j), xbuf_ref.at[slot], in_sem.at[slot])

    def out_copy(j, slot):
        return pltpu.make_async_copy(ybuf_ref.at[slot], rows_of(ys_ref, j), out_sem.at[slot])

    @pl.when(n_blk > 0)
    def _():
        in_copy(0, 0).start(priority=_ROW_DMA_PRIORITY)
        w1b_ref[...] = w1_ref[0, 0].astype(BF16)
        w2b_ref[...] = w2_ref[0, 0].astype(BF16)

    def block(j, c):
        slot = j % 2

        @pl.when(j + 1 < n_blk)
        def _():
            in_copy(j + 1, 1 - slot).start(priority=_ROW_DMA_PRIORITY)

        in_copy(j, slot).wait()

        @pl.when(j >= 2)
        def _():
            out_copy(j - 2, slot).wait()

        f = w2b_ref.shape[0]
        x = _load_slabs(xbuf_ref.at[slot], MOE_BLOCK).astype(BF16)
        h = jnp.dot(x, w1b_ref[...], preferred_element_type=F32) + b1_ref[0, 0]
        glu = jnp.minimum(h[:, :f], SWIGLU_LIMIT)
        lin = jnp.clip(h[:, f:], -SWIGLU_LIMIT, SWIGLU_LIMIT)
        act = glu / (1.0 + jnp.exp(-SWIGLU_ALPHA * glu)) * (lin + 1.0)
        y = jnp.dot(act.astype(BF16), w2b_ref[...], preferred_element_type=F32) + b2_ref[0, 0]
        _store_slabs(ybuf_ref.at[slot], y)
        out_copy(j, slot).start(priority=_ROW_DMA_PRIORITY)
        return c

    lax.fori_loop(0, n_blk, block, 0)

    @pl.when(n_blk >= 2)
    def _():
        out_copy(n_blk - 2, n_blk % 2).wait()

    @pl.when(n_blk >= 1)
    def _():
        out_copy(n_blk - 1, (n_blk - 1) % 2).wait()


def _expert_ffn(first_blk, n_blk, xs, layer, w1, b1, w2, b2):
    depth, n_exp, d, f2 = w1.shape
    f = w2.shape[2]
    by_expert = lambda e, fb, nb: (layer, e, 0, 0)
    block_rows = MOE_BLOCK * _SUBLANES
    return pl.pallas_call(
        _ffn_kernel,
        grid_spec=pltpu.PrefetchScalarGridSpec(
            num_scalar_prefetch=2,
            grid=(n_exp,),
            in_specs=[pl.BlockSpec(memory_space=pl.ANY),
                      pl.BlockSpec((1, 1, d, f2), by_expert), pl.BlockSpec((1, 1, 1, f2), by_expert),
                      pl.BlockSpec((1, 1, f, d), by_expert), pl.BlockSpec((1, 1, 1, d), by_expert)],
            out_specs=pl.BlockSpec(memory_space=pl.ANY),
            scratch_shapes=[pltpu.VMEM((d, f2), BF16), pltpu.VMEM((f, d), BF16),
                            pltpu.VMEM((2, block_rows, _LANES), F32), pltpu.VMEM((2, block_rows, _LANES), F32),
                            pltpu.SemaphoreType.DMA((2,)), pltpu.SemaphoreType.DMA((2,))]),
        out_shape=jax.ShapeDtypeStruct(xs.shape, F32),
        input_output_aliases={2: 0},
        compiler_params=_params("arbitrary"),
        name="moe_expert_ffn",
    )(first_blk, n_blk, xs, w1, b1.reshape(depth, n_exp, 1, f2), w2, b2.reshape(depth, n_exp, 1, d))


def _combine_kernel(src_ref, nch_ref, off_ref, tot_ref, ys_ref, gate_ref, eidx_ref, rank_ref, offs_ref, x1_ref,
                    g_ref, b_ref, x2_ref, x2b_ref, buf_ref, sem, *, n_exp):
    i = pl.program_id(0)
    tm = x1_ref.shape[0]
    n_buf = buf_ref.shape[0] // _SUBLANES

    @pl.when(i == 0)
    def _():
        buf_ref[...] = jnp.zeros_like(buf_ref)

    def copy(sorted_row, buffer_row):
        return pltpu.make_async_copy(_chunk_slabs(ys_ref, sorted_row), _chunk_slabs(buf_ref, buffer_row), sem)

    _for_each_chunk(i, n_exp, src_ref, nch_ref, off_ref, lambda s, b: copy(s, b).start())
    buf_col = lax.broadcasted_iota(I32, (tm, n_buf), 1).astype(F32)
    weights = jnp.zeros((tm, n_buf), F32)
    for k, p in enumerate(_buffer_positions(eidx_ref, rank_ref, offs_ref)):
        weights = jnp.where(buf_col == p, gate_ref[:, k:k + 1], weights)
    _wait_chunks(tot_ref[i], copy(0, 0))
    y = jnp.dot(weights.astype(BF16), _load_slabs(buf_ref, n_buf).astype(BF16), preferred_element_type=F32)
    x2 = _layer_norm(DEEPNORM_ALPHA * x1_ref[...] + y, g_ref[...], b_ref[...])
    x2_ref[...] = x2
    x2b_ref[...] = x2.astype(BF16)


def _combine(tables, ys, gates, eidx, rank, offs, x1, ln_g, ln_b):
    t, d = x1.shape
    n_exp = N_EXPERTS
    tm = min(t, _MOE_TILE)
    row = lambda i, *_: (i, 0)
    fixed = lambda i, *_: (0, 0)
    return pl.pallas_call(
        functools.partial(_combine_kernel, n_exp=n_exp),
        grid_spec=pltpu.PrefetchScalarGridSpec(
            num_scalar_prefetch=4,
            grid=(t // tm,),
            in_specs=[pl.BlockSpec(memory_space=pl.ANY), pl.BlockSpec((tm, _LANES), row),
                      pl.BlockSpec((tm, _LANES), row), pl.BlockSpec((tm, _LANES), row),
                      pl.BlockSpec((1, _SUBLANES, _LANES), lambda i, *_: (i, 0, 0)),
                      pl.BlockSpec((tm, d), row), pl.BlockSpec((1, d), fixed), pl.BlockSpec((1, d), fixed)],
            out_specs=[pl.BlockSpec((tm, d), row), pl.BlockSpec((tm, d), row)],
            scratch_shapes=[pltpu.VMEM((_run_rows(tm, n_exp) * _SUBLANES, _LANES), F32),
                            pltpu.SemaphoreType.DMA(())]),
        out_shape=[jax.ShapeDtypeStruct((t, d), F32), jax.ShapeDtypeStruct((t, d), BF16)],
        compiler_params=_params("arbitrary"),
        name="moe_combine",
    )(*tables, ys, gates, eidx, rank, offs, x1, ln_g.reshape(1, d), ln_b.reshape(1, d))


def _moe_layer(x1, x1b, eidx, gates, rank, starts, counts, layer, w1, b1, w2, b2, ln_g, ln_b):
    t = x1.shape[0]
    n_exp = w1.shape[1]
    slack = _RUN_CHUNK - 1
    n_blocks = -(-(t * TOP_K + n_exp * slack) // MOE_BLOCK) + n_exp
    cnt = counts[0, :n_exp]
    padded = jnp.where(cnt > 0, ((cnt + slack + MOE_BLOCK - 1) // MOE_BLOCK) * MOE_BLOCK, 0)
    pad_ends = jnp.cumsum(padded)
    pad_starts = pad_ends - padded
    run_start = starts[:, 0, :n_exp]
    run_len = jnp.concatenate([run_start[1:], cnt[None]], axis=0) - run_start
    n_chunks = (run_len + slack) // _RUN_CHUNK
    buf_off = (jnp.cumsum(n_chunks, axis=1) - n_chunks) * _RUN_CHUNK
    src = pad_starts[None, :] + run_start
    tables = (src.astype(I32).reshape(-1), n_chunks.astype(I32).reshape(-1),
              buf_off.astype(I32).reshape(-1), jnp.sum(n_chunks, axis=1).astype(I32))
    offs = jnp.zeros(starts.shape, I32).at[:, :, :n_exp].set(buf_off[:, None, :].astype(I32))
    tail = pad_ends - (src[-1] + n_chunks[-1] * _RUN_CHUNK)
    z_n = jnp.concatenate([(tail + slack) // _RUN_CHUNK, (n_blocks * MOE_BLOCK - pad_ends[-1:]) // _RUN_CHUNK])
    z_src = jnp.concatenate([pad_ends, pad_ends[-1:]]) - z_n * _RUN_CHUNK
    z_src = z_src.at[-1].set(pad_ends[-1])
    zero_tables = (z_src.astype(I32), z_n.astype(I32), jnp.sum(z_n, keepdims=True).astype(I32))
    xs = _dispatch(tables, zero_tables, x1b, eidx, rank, offs, n_blocks * MOE_BLOCK)
    ys = _expert_ffn((pad_starts // MOE_BLOCK).astype(I32), (padded // MOE_BLOCK).astype(I32), xs,
                     layer, w1, b1, w2, b2)
    return _combine(tables, ys, gates, eidx, rank, offs, x1, ln_g, ln_b)


def _fox_mixer(x2, xb, w_in, b_f, batch, seq):
    d = x2.shape[1]
    qkv = _matmul(xb, w_in[:, :3 * d].astype(BF16), BF16)
    c_pad = _fox_gate_cumsum(x2, w_in[:, 3 * d:], b_f, batch, seq)
    return _fox_attention(qkv, c_pad, batch, seq, FOX_HEADS)


def _moba_mixer(xb, w_in, batch, seq):
    return _moba_attention(_matmul(xb, w_in.astype(BF16), BF16), batch, seq, MOBA_HEADS)


def _retention_mixer(xb, w_in, gn_g, batch, seq):
    return _retention(_matmul(xb, w_in.astype(BF16), BF16), gn_g, batch, seq, RET_HEADS)


def kernel(x, fox_w_in, fox_b_f, fox_w_out, moba_w_in, moba_w_out, ret_w_in, ret_gn_g, ret_w_out, ln_g, ln_b,
           router_w, router_b, moe_w1, moe_b1, moe_w2, moe_b2):
    batch, seq, d = x.shape
    x2 = x.reshape(batch * seq, d)
    xb = x2.astype(BF16)
    for i in range(DEPTH):
        kind, j = i % 3, i // 3
        if kind == 0:
            o, w_out = _fox_mixer(x2, xb, fox_w_in[j], fox_b_f[j], batch, seq), fox_w_out[j]
        elif kind == 1:
            o, w_out = _moba_mixer(xb, moba_w_in[j], batch, seq), moba_w_out[j]
        else:
            o, w_out = _retention_mixer(xb, ret_w_in[j], ret_gn_g[j], batch, seq), ret_w_out[j]
        routed = _post_mixer(o, w_out.astype(BF16), x2, ln_g[i, 0], ln_b[i, 0], router_w[i], router_b[i])
        x2, xb = _moe_layer(*routed, i, moe_w1, moe_b1, moe_w2, moe_b2, ln_g[i, 1], ln_b[i, 1])
    return x2.reshape(batch, seq, d)
```

```python
import functools

import jax
import jax.numpy as jnp
from jax import lax
from jax.experimental import pallas as pl
from jax.experimental.pallas import tpu as pltpu

F32, BF16, I32 = jnp.float32, jnp.bfloat16, jnp.int32
_HIGHEST = lax.Precision.HIGHEST
_NT = (((1,), (1,)), ((), ()))
_LANES = 128
_SUBLANES = 8
_BF16_ROWS = 16
_MASKED = -1e30
_BIAS_OFF = -32768.0
_LOG2E = 1.4426950408889634
_VMEM_LIMIT = 56 * 1024 * 1024
_MOE_TILE = 512
_RUN_CHUNK = 16
_ROW_DMA_PRIORITY = 1

DEPTH = 4
FOX_HEADS, MOBA_HEADS, RET_HEADS = 8, 8, 4
MOBA_BLOCK, MOBA_TOPK = 256, 3
RET_ROPE_BASE, RET_GN_EPS = 10000.0, 1e-6
N_EXPERTS, TOP_K, MOE_BLOCK = 32, 4, 256
SWIGLU_LIMIT, SWIGLU_ALPHA = 7.0, 1.702
LN_EPS = 1e-5
DEEPNORM_ALPHA = (2 * DEPTH) ** 0.25


def _params(*sem):
    return pltpu.CompilerParams(dimension_semantics=sem, vmem_limit_bytes=_VMEM_LIMIT)


def _layer_norm(y, g, b):
    mu = jnp.mean(y, axis=-1, keepdims=True)
    d = y - mu
    var = jnp.mean(d * d, axis=-1, keepdims=True)
    return d * lax.rsqrt(var + LN_EPS) * g + b


def _mm_kernel(a_ref, w_ref, o_ref):
    o_ref[...] = jnp.dot(a_ref[...], w_ref[...], preferred_element_type=F32).astype(o_ref.dtype)


def _matmul(a, w, out_dtype):
    m, k = a.shape
    n = w.shape[1]
    tm, tn = min(m, 1024), min(n, 1024)
    return pl.pallas_call(
        _mm_kernel,
        grid=(m // tm, n // tn),
        in_specs=[pl.BlockSpec((tm, k), lambda i, j: (i, 0)),
                  pl.BlockSpec((k, tn), lambda i, j: (0, j))],
        out_specs=pl.BlockSpec((tm, tn), lambda i, j: (i, j)),
        out_shape=jax.ShapeDtypeStruct((m, n), out_dtype),
        compiler_params=_params("parallel", "parallel"),
        name="proj_matmul",
    )(a, w)


def _flash_first(s, v):
    m = jnp.max(s, axis=1, keepdims=True)
    p = jnp.exp2(s - m)
    l = jnp.sum(p, axis=1, keepdims=True)
    acc = jnp.dot(p.astype(BF16), v, preferred_element_type=F32)
    return m, l, acc


def _flash_next(carry, s, v):
    m, l, acc = carry
    m_new = jnp.maximum(m, jnp.max(s, axis=1, keepdims=True))
    a = jnp.exp2(m - m_new)
    p = jnp.exp2(s - m_new)
    l = a * l + jnp.sum(p, axis=1, keepdims=True)
    acc = a * acc + jnp.dot(p.astype(BF16), v, preferred_element_type=F32)
    return m_new, l, acc


def _causal_flash(q_aug, kaug_ref, v_ref, i, tq):
    def scores(j, width):
        start = pl.multiple_of(j * tq, tq)
        return lax.dot_general(q_aug, kaug_ref[pl.ds(start, width), :], _NT, preferred_element_type=F32)

    def values(j, width):
        return v_ref[pl.ds(pl.multiple_of(j * tq, tq), width), :]

    row = lax.broadcasted_iota(I32, (tq, tq), 0)
    col = lax.broadcasted_iota(I32, (tq, tq), 1)
    carry = _flash_first(jnp.where(col <= row, scores(i, tq), _MASKED), values(i, tq))
    carry = lax.fori_loop(
        0, i // 2, lambda p, c: _flash_next(c, scores(2 * p, 2 * tq), values(2 * p, 2 * tq)), carry)
    _, l, acc = lax.cond(i % 2 == 1, lambda c: _flash_next(c, scores(i - 1, tq), values(i - 1, tq)),
                         lambda c: c, carry)
    return acc / l


def _lane_columns(n, cols):
    lane = lax.broadcasted_iota(I32, (n, _LANES), 1)
    out = jnp.zeros((n, _LANES), F32)
    for t, c in enumerate(cols):
        out = jnp.where(lane == t, c, out)
    return out.astype(BF16)


def _split3(c):
    hi = c.astype(BF16).astype(F32)
    r = c - hi
    mid = r.astype(BF16).astype(F32)
    lo = (r - mid).astype(BF16).astype(F32)
    return [hi, mid, lo]


def _fox_gate_kernel(x_ref, wf_ref, bf_ref, c_ref, carry_ref):
    @pl.when(pl.program_id(1) == 0)
    def _():
        carry_ref[...] = jnp.zeros_like(carry_ref)

    x = x_ref[...]
    x_hi = x.astype(BF16)
    x_lo = (x - x_hi.astype(F32)).astype(BF16)
    hi_part = jnp.dot(x_hi, wf_ref[...], preferred_element_type=F32)
    lo_part = jnp.dot(x_lo, wf_ref[:, :_LANES], preferred_element_type=F32)
    z = hi_part[:, :_LANES] + hi_part[:, _LANES:] + lo_part + bf_ref[...]
    log_f = jnp.minimum(z, 0.0) - jnp.log1p(jnp.exp(-jnp.abs(z)))
    tc = z.shape[0]
    row = lax.broadcasted_iota(I32, (tc, tc), 0)
    col = lax.broadcasted_iota(I32, (tc, tc), 1)
    tri = jnp.where(col <= row, 1.0, 0.0).astype(BF16)
    hi, mid, lo = [p.astype(BF16) for p in _split3(log_f)]
    hi_mid = jnp.dot(tri, jnp.concatenate([hi, mid], axis=1), preferred_element_type=F32)
    c = (hi_mid[:, :_LANES] + hi_mid[:, _LANES:] + jnp.dot(tri, lo, preferred_element_type=F32)
         + carry_ref[...])
    c_ref[...] = c
    carry_ref[...] = c[tc - 1:tc, :]


def _fox_gate_cumsum(x2, w_f, b_f, batch, seq):
    t, d = x2.shape
    h = w_f.shape[1]
    wf = jnp.zeros((d, _LANES), F32).at[:, :h].set(w_f)
    wf_hi = wf.astype(BF16)
    wf = jnp.concatenate([wf_hi, (wf - wf_hi.astype(F32)).astype(BF16)], axis=1)
    bf = jnp.zeros((1, _LANES), F32).at[0, :h].set(b_f)
    tc = min(seq, 512)
    ns = seq // tc
    return pl.pallas_call(
        _fox_gate_kernel,
        grid=(batch, ns),
        in_specs=[pl.BlockSpec((tc, d), lambda b, s: (b * ns + s, 0)),
                  pl.BlockSpec((d, 2 * _LANES), lambda b, s: (0, 0)),
                  pl.BlockSpec((1, _LANES), lambda b, s: (0, 0))],
        out_specs=pl.BlockSpec((tc, _LANES), lambda b, s: (b * ns + s, 0)),
        out_shape=jax.ShapeDtypeStruct((t, _LANES), F32),
        scratch_shapes=[pltpu.VMEM((1, _LANES), F32)],
        compiler_params=_params("parallel", "arbitrary"),
        name="fox_gate_cumsum",
    )(x2, wf, bf)


def _fox_attn_kernel(q_ref, k_ref, v_ref, cq_ref, ck_ref, o_ref, kaug_ref, *, scale):
    h = pl.program_id(1)
    i = pl.program_id(2)
    tq, hd = q_ref.shape
    seq = k_ref.shape[0]

    def head_column(c_ref, rows):
        lane = lax.broadcasted_iota(I32, (rows, _LANES), 1)
        return jnp.sum(jnp.where(lane == h, c_ref[...], 0.0), axis=1, keepdims=True) * _LOG2E

    @pl.when(i == 0)
    def _():
        kaug_ref[:, :hd] = k_ref[...]
        kaug_ref[:, hd:] = _lane_columns(seq, [1.0, 1.0, 1.0] + [-c for c in _split3(head_column(ck_ref, seq))])

    q = (q_ref[...].astype(F32) * (scale * _LOG2E)).astype(BF16)
    q_aug = jnp.concatenate([q, _lane_columns(tq, _split3(head_column(cq_ref, tq)) + [1.0, 1.0, 1.0])], axis=1)
    o_ref[...] = _causal_flash(q_aug, kaug_ref, v_ref, i, tq).astype(o_ref.dtype)


def _fox_attention(qkv, c_pad, batch, seq, heads):
    t = qkv.shape[0]
    d = qkv.shape[1] // 3
    hd = d // heads
    tq = min(seq, 512)
    nq = seq // tq
    return pl.pallas_call(
        functools.partial(_fox_attn_kernel, scale=hd ** -0.5),
        grid=(batch, heads, nq),
        in_specs=[pl.BlockSpec((tq, hd), lambda b, h, i: (b * nq + i, h)),
                  pl.BlockSpec((seq, hd), lambda b, h, i: (b, heads + h)),
                  pl.BlockSpec((seq, hd), lambda b, h, i: (b, 2 * heads + h)),
                  pl.BlockSpec((tq, _LANES), lambda b, h, i: (b * nq + i, 0)),
                  pl.BlockSpec((seq, _LANES), lambda b, h, i: (b, 0))],
        out_specs=pl.BlockSpec((tq, hd), lambda b, h, i: (b * nq + i, h)),
        out_shape=jax.ShapeDtypeStruct((t, d), BF16),
        scratch_shapes=[pltpu.VMEM((seq, hd + _LANES), BF16)],
        compiler_params=_params("parallel", "parallel", "arbitrary"),
        name="fox_attention",
    )(qkv, qkv, qkv, c_pad, c_pad)


def _moba_attn_kernel(q_ref, k_ref, v_ref, o_ref, kaug_ref, kmean_ref, *, scale, blk, topk):
    i = pl.program_id(2)
    tq, hd = q_ref.shape
    seq = k_ref.shape[0]
    nb = seq // blk

    @pl.when(i == 0)
    def _():
        kmean_ref[...] = jnp.zeros_like(kmean_ref)
        for n in range(nb):
            kb = k_ref[n * blk:(n + 1) * blk, :].astype(F32)
            kmean_ref[n:n + 1, :] = jnp.mean(kb, axis=0, keepdims=True)
        kaug_ref[:, :hd] = k_ref[...]
        key_block = lax.broadcasted_iota(I32, (seq, _LANES), 0) // blk
        lane = lax.broadcasted_iota(I32, (seq, _LANES), 1)
        kaug_ref[:, hd:] = jnp.where(key_block == lane, 1.0, 0.0).astype(BF16)

    q = q_ref[...]
    gate = lax.dot_general(q.astype(F32), kmean_ref[...], _NT, precision=_HIGHEST,
                           preferred_element_type=F32)
    lane = lax.broadcasted_iota(I32, gate.shape, 1)
    lane_f = lane.astype(F32)
    own = (i * tq + lax.broadcasted_iota(I32, gate.shape, 0)) // blk
    gate = jnp.where(lane < own, gate, -jnp.inf)
    allowed = lane == own
    for r in range(topk):
        mx = jnp.max(gate, axis=1, keepdims=True)
        idx = jnp.min(jnp.where(gate == mx, lane_f, float(_LANES)), axis=1, keepdims=True)
        pick = lane_f == idx
        allowed = jnp.logical_or(allowed, jnp.logical_and(pick, r < own))
        gate = jnp.where(pick, -jnp.inf, gate)
    bias = jnp.where(allowed, 0.0, _BIAS_OFF).astype(BF16)
    q_aug = jnp.concatenate([(q.astype(F32) * (scale * _LOG2E)).astype(BF16), bias], axis=1)
    o_ref[...] = _causal_flash(q_aug, kaug_ref, v_ref, i, tq).astype(o_ref.dtype)


def _moba_attention(qkv, batch, seq, heads):
    t = qkv.shape[0]
    d = qkv.shape[1] // 3
    hd = d // heads
    blk = MOBA_BLOCK
    assert seq % blk == 0 and seq // blk <= _LANES
    tq = 2 * blk if seq % (2 * blk) == 0 else blk
    nq = seq // tq
    return pl.pallas_call(
        functools.partial(_moba_attn_kernel, scale=hd ** -0.5, blk=blk, topk=min(MOBA_TOPK, seq // blk)),
        grid=(batch, heads, nq),
        in_specs=[pl.BlockSpec((tq, hd), lambda b, h, i: (b * nq + i, h)),
                  pl.BlockSpec((seq, hd), lambda b, h, i: (b, heads + h)),
                  pl.BlockSpec((seq, hd), lambda b, h, i: (b, 2 * heads + h))],
        out_specs=pl.BlockSpec((tq, hd), lambda b, h, i: (b * nq + i, h)),
        out_shape=jax.ShapeDtypeStruct((t, d), BF16),
        scratch_shapes=[pltpu.VMEM((seq, hd + _LANES), BF16), pltpu.VMEM((_LANES, hd), F32)],
        compiler_params=_params("parallel", "parallel", "arbitrary"),
        name="moba_attention",
    )(qkv, qkv, qkv)


def _retention_kernel(q_ref, k_ref, v_ref, g_ref, cos_ref, sin_ref, dm_ref, xi_ref, zeta_ref, gc_ref,
                      gn_ref, o_ref, state_ref, *, k_scale):
    @pl.when(pl.program_id(2) == 0)
    def _():
        state_ref[...] = jnp.zeros_like(state_ref)

    cos, sin = cos_ref[...], sin_ref[...]
    half = cos.shape[1]

    def rotate(t):
        t = t.astype(F32)
        t1, t2 = t[:, :half], t[:, half:]
        return jnp.concatenate([t1 * cos - t2 * sin, t1 * sin + t2 * cos], axis=1)

    q = rotate(q_ref[...]).astype(BF16)
    k_rot = rotate(k_ref[...]) * k_scale
    k = k_rot.astype(BF16)
    k_t = k_rot.T.astype(BF16)
    v = v_ref[...]
    s = lax.dot_general(q, k, _NT, preferred_element_type=F32) * dm_ref[0]
    inner = jnp.dot(s.astype(BF16), v, preferred_element_type=F32)
    state = state_ref[...]
    cross = jnp.dot(q, state.astype(BF16), preferred_element_type=F32) * xi_ref[0]
    o = inner + cross
    vz = (v.astype(F32) * zeta_ref[0]).astype(BF16)
    state_ref[...] = gc_ref[0][:, :1] * state + jnp.dot(k_t, vz, preferred_element_type=F32)

    mu = jnp.mean(o, axis=-1, keepdims=True)
    d = o - mu
    var = jnp.mean(d * d, axis=-1, keepdims=True)
    o = d * lax.rsqrt(var + RET_GN_EPS) * gn_ref[...]
    g = g_ref[...].astype(F32)
    o_ref[...] = (g / (1.0 + jnp.exp(-g)) * o).astype(o_ref.dtype)


def _retention(proj, gn_g, batch, seq, heads):
    t, width = proj.shape
    dk = width // (6 * heads)
    dv = 2 * dk
    chunk = min(seq, 256)
    nc = seq // chunk
    pos = jnp.arange(seq, dtype=F32)
    inv_freq = jnp.exp(-jnp.log(RET_ROPE_BASE) * jnp.arange(0, dk, 2, dtype=F32) / dk)
    ang = pos[:, None] * inv_freq[None, :]
    cos, sin = jnp.cos(ang), jnp.sin(ang)
    log_gamma = jnp.log(1.0 - jnp.exp2(-5.0 - jnp.arange(heads, dtype=F32)))
    n = jnp.arange(chunk, dtype=F32)
    diff = n[:, None] - n[None, :]
    d_mask = jnp.where(diff[None] >= 0, jnp.exp(diff[None] * log_gamma[:, None, None]), 0.0)
    xi = jnp.exp((n[None, :] + 1.0) * log_gamma[:, None])[:, :, None]
    zeta = jnp.exp((chunk - 1.0 - n[None, :]) * log_gamma[:, None])[:, :, None]
    g_chunk = jnp.broadcast_to(jnp.exp(chunk * log_gamma)[:, None, None], (heads, 1, _LANES))
    return pl.pallas_call(
        functools.partial(_retention_kernel, k_scale=dk ** -0.5),
        grid=(batch, heads, nc),
        in_specs=[pl.BlockSpec((chunk, dk), lambda b, h, c: (b * nc + c, h)),
                  pl.BlockSpec((chunk, dk), lambda b, h, c: (b * nc + c, heads + h)),
                  pl.BlockSpec((chunk, dv), lambda b, h, c: (b * nc + c, heads + h)),
                  pl.BlockSpec((chunk, dv), lambda b, h, c: (b * nc + c, 2 * heads + h)),
                  pl.BlockSpec((chunk, dk // 2), lambda b, h, c: (c, 0)),
                  pl.BlockSpec((chunk, dk // 2), lambda b, h, c: (c, 0)),
                  pl.BlockSpec((1, chunk, chunk), lambda b, h, c: (h, 0, 0)),
                  pl.BlockSpec((1, chunk, 1), lambda b, h, c: (h, 0, 0)),
                  pl.BlockSpec((1, chunk, 1), lambda b, h, c: (h, 0, 0)),
                  pl.BlockSpec((1, 1, _LANES), lambda b, h, c: (h, 0, 0)),
                  pl.BlockSpec((1, dv), lambda b, h, c: (0, h))],
        out_specs=pl.BlockSpec((chunk, dv), lambda b, h, c: (b * nc + c, h)),
        out_shape=jax.ShapeDtypeStruct((t, heads * dv), BF16),
        scratch_shapes=[pltpu.VMEM((dk, dv), F32)],
        compiler_params=_params("parallel", "parallel", "arbitrary"),
        name="retention",
    )(proj, proj, proj, proj, cos, sin, d_mask, xi, zeta, g_chunk, gn_g.reshape(1, -1))


def _post_mixer_kernel(o_ref, w_ref, x_ref, g_ref, b_ref, rw_ref, rb_ref,
                       x1_ref, x1b_ref, eidx_ref, gate_ref, rank_ref, start_ref, cnt_ref, run_ref, *, n_exp, topk):
    @pl.when(pl.program_id(0) == 0)
    def _():
        run_ref[...] = jnp.zeros_like(run_ref)

    h = jnp.dot(o_ref[...], w_ref[...], preferred_element_type=F32)
    x1 = _layer_norm(DEEPNORM_ALPHA * x_ref[...] + h, g_ref[...], b_ref[...])
    x1_ref[...] = x1
    x1b_ref[...] = x1.astype(BF16)
    start_ref[...] = jnp.broadcast_to(run_ref[...], start_ref.shape[1:]).astype(I32)[None]

    x_hi = x1.astype(BF16)
    x_lo = (x1 - x_hi.astype(F32)).astype(BF16)
    hi_part = jnp.dot(x_hi, rw_ref[...], preferred_element_type=F32)
    lo_part = jnp.dot(x_lo, rw_ref[:, :_LANES], preferred_element_type=F32)
    logits = hi_part[:, :_LANES] + hi_part[:, _LANES:] + lo_part + rb_ref[...]
    tm = logits.shape[0]
    lane = lax.broadcasted_iota(I32, logits.shape, 1).astype(F32)
    logits = jnp.where(lane < n_exp, logits, -jnp.inf)
    vals, idxs = [], []
    for _ in range(topk):
        mx = jnp.max(logits, axis=1, keepdims=True)
        idx = jnp.min(jnp.where(logits == mx, lane, float(_LANES)), axis=1, keepdims=True)
        vals.append(mx)
        idxs.append(idx)
        logits = jnp.where(lane == idx, -jnp.inf, logits)
    exps = [jnp.exp(v - vals[0]) for v in vals]
    den = exps[0]
    for e in exps[1:]:
        den = den + e

    picked = jnp.zeros(logits.shape, F32)
    for idx in idxs:
        picked = jnp.where(lane == idx, 1.0, picked)
    row = lax.broadcasted_iota(I32, (tm, tm), 0)
    col = lax.broadcasted_iota(I32, (tm, tm), 1)
    before = (col < row).astype(BF16)
    prefix = jnp.dot(before, picked.astype(BF16), preferred_element_type=F32)

    eidx = jnp.zeros(logits.shape, F32)
    gates = jnp.zeros(logits.shape, F32)
    ranks = jnp.zeros(logits.shape, F32)
    for r in range(topk):
        rank_r = jnp.sum(jnp.where(lane == idxs[r], prefix, 0.0), axis=1, keepdims=True)
        eidx = jnp.where(lane == r, idxs[r], eidx)
        gates = jnp.where(lane == r, exps[r] / den, gates)
        ranks = jnp.where(lane == r, rank_r, ranks)
    eidx_ref[...] = eidx.astype(I32)
    gate_ref[...] = gates
    rank_ref[...] = ranks.astype(I32)
    run = run_ref[...] + jnp.sum(picked, axis=0, keepdims=True)
    run_ref[...] = run
    cnt_ref[...] = jnp.broadcast_to(run, cnt_ref.shape).astype(I32)


def _post_mixer(o, w_out, x2, ln_g, ln_b, router_w, router_b):
    t, d = x2.shape
    assert d == _SUBLANES * _LANES
    kdim = o.shape[1]
    n_exp = router_w.shape[1]
    tm = min(t, _MOE_TILE)
    rw = jnp.zeros((d, _LANES), F32).at[:, :n_exp].set(router_w)
    rw_hi = rw.astype(BF16)
    rw = jnp.concatenate([rw_hi, (rw - rw_hi.astype(F32)).astype(BF16)], axis=1)
    rb = jnp.zeros((1, _LANES), F32).at[0, :n_exp].set(router_b)
    row = lambda i: (i, 0)
    fixed = lambda i: (0, 0)
    return pl.pallas_call(
        functools.partial(_post_mixer_kernel, n_exp=n_exp, topk=TOP_K),
        grid=(t // tm,),
        in_specs=[pl.BlockSpec((tm, kdim), row), pl.BlockSpec((kdim, d), fixed),
                  pl.BlockSpec((tm, d), row), pl.BlockSpec((1, d), fixed), pl.BlockSpec((1, d), fixed),
                  pl.BlockSpec((d, 2 * _LANES), fixed), pl.BlockSpec((1, _LANES), fixed)],
        out_specs=[pl.BlockSpec((tm, d), row), pl.BlockSpec((tm, d), row), pl.BlockSpec((tm, _LANES), row),
                   pl.BlockSpec((tm, _LANES), row), pl.BlockSpec((tm, _LANES), row),
                   pl.BlockSpec((1, _SUBLANES, _LANES), lambda i: (i, 0, 0)),
                   pl.BlockSpec((_SUBLANES, _LANES), fixed)],
        out_shape=[jax.ShapeDtypeStruct((t, d), F32), jax.ShapeDtypeStruct((t, d), BF16),
                   jax.ShapeDtypeStruct((t, _LANES), I32), jax.ShapeDtypeStruct((t, _LANES), F32),
                   jax.ShapeDtypeStruct((t, _LANES), I32),
                   jax.ShapeDtypeStruct((t // tm, _SUBLANES, _LANES), I32),
                   jax.ShapeDtypeStruct((_SUBLANES, _LANES), I32)],
        scratch_shapes=[pltpu.VMEM((1, _LANES), F32)],
        compiler_params=_params("arbitrary"),
        name="post_mixer",
    )(o, w_out, x2, ln_g.reshape(1, d), ln_b.reshape(1, d), rw, rb)


def _run_rows(tile_tokens, n_exp):
    rows = tile_tokens * TOP_K + n_exp * (_RUN_CHUNK - 1)
    return -(-rows // _LANES) * _LANES


def _buffer_positions(eidx_ref, rank_ref, offs_ref):
    lane = lax.broadcasted_iota(I32, eidx_ref.shape, 1)
    offs = offs_ref[0, 0:1, :].astype(F32)
    pos = []
    for k in range(TOP_K):
        start = jnp.sum(jnp.where(lane == eidx_ref[:, k:k + 1], offs, 0.0), axis=1, keepdims=True)
        pos.append(start + rank_ref[:, k:k + 1].astype(F32))
    return pos


def _for_each_chunk(i, n_exp, src_ref, nch_ref, off_ref, fn):
    def expert(e, c):
        t = i * n_exp + e

        def chunk(j, c2):
            fn(src_ref[t] + j * _RUN_CHUNK, off_ref[t] + j * _RUN_CHUNK)
            return c2

        return lax.fori_loop(0, nch_ref[t], chunk, c)

    lax.fori_loop(0, n_exp, expert, 0)


def _sorted_chunk(ref, row):
    return ref.at[pl.ds(pl.multiple_of(row, _SUBLANES), _RUN_CHUNK)]


def _buffer_chunk(ref, row):
    return ref.at[pl.ds(pl.multiple_of(row, _BF16_ROWS), _RUN_CHUNK)]


def _dispatch_kernel(src_ref, nch_ref, off_ref, tot_ref, zsrc_ref, zn_ref, ztot_ref, xb_ref, eidx_ref, rank_ref,
                     offs_ref, xs_ref, buf_ref, zero_ref, sem, *, n_exp):
    i = pl.program_id(0)
    tm = xb_ref.shape[0]
    n_buf = buf_ref.shape[0]

    @pl.when(i == 0)
    def _():
        zero_ref[...] = jnp.zeros_like(zero_ref)

        def zero_copy(sorted_row):
            return pltpu.make_async_copy(zero_ref, _sorted_chunk(xs_ref, sorted_row), sem)

        def region(e, c):
            def chunk(j, c2):
                zero_copy(zsrc_ref[e] + j * _RUN_CHUNK).start()
                return c2

            return lax.fori_loop(0, zn_ref[e], chunk, c)

        lax.fori_loop(0, n_exp + 1, region, 0)
        _wait_chunks(ztot_ref[0], zero_copy(0))

    lane = lax.broadcasted_iota(I32, eidx_ref.shape, 1)
    pos = jnp.full(eidx_ref.shape, -1.0, F32)
    for k, p in enumerate(_buffer_positions(eidx_ref, rank_ref, offs_ref)):
        pos = jnp.where(lane == k, p, pos)
    pos_t = pos.T
    buf_row = lax.broadcasted_iota(I32, (n_buf, tm), 0).astype(F32)
    place = buf_row == pos_t[0:1, :]
    for k in range(1, TOP_K):
        place = jnp.logical_or(place, buf_row == pos_t[k:k + 1, :])
    rows = jnp.dot(jnp.where(place, 1.0, 0.0).astype(BF16), xb_ref[...], preferred_element_type=F32)
    buf_ref[...] = rows.astype(BF16)

    def copy(sorted_row, buffer_row):
        return pltpu.make_async_copy(_buffer_chunk(buf_ref, buffer_row), _sorted_chunk(xs_ref, sorted_row), sem)

    _for_each_chunk(i, n_exp, src_ref, nch_ref, off_ref, lambda s, b: copy(s, b).start())
    _wait_chunks(tot_ref[i], copy(0, 0))


def _wait_chunks(n, same_size_copy):
    def wait(j, c):
        same_size_copy.wait()
        return c

    lax.fori_loop(0, n, wait, 0)


def _dispatch(tables, zero_tables, x1b, eidx, rank, offs, n_rows):
    t, d = x1b.shape
    n_exp = N_EXPERTS
    tm = min(t, _MOE_TILE)
    row = lambda i, *_: (i, 0)
    return pl.pallas_call(
        functools.partial(_dispatch_kernel, n_exp=n_exp),
        grid_spec=pltpu.PrefetchScalarGridSpec(
            num_scalar_prefetch=7,
            grid=(t // tm,),
            in_specs=[pl.BlockSpec((tm, d), row), pl.BlockSpec((tm, _LANES), row), pl.BlockSpec((tm, _LANES), row),
                      pl.BlockSpec((1, _SUBLANES, _LANES), lambda i, *_: (i, 0, 0))],
            out_specs=pl.BlockSpec(memory_space=pl.ANY),
            scratch_shapes=[pltpu.VMEM((_run_rows(tm, n_exp), d), BF16), pltpu.VMEM((_RUN_CHUNK, d), BF16),
                            pltpu.SemaphoreType.DMA(())]),
        out_shape=jax.ShapeDtypeStruct((n_rows, d), BF16),
        compiler_params=_params("arbitrary"),
        name="moe_dispatch",
    )(*tables, *zero_tables, x1b, eidx, rank, offs)


def _ffn_kernel(first_blk_ref, n_blk_ref, xs_ref, w1_ref, b1_ref, w2_ref, b2_ref, ys_ref,
                w1b_ref, w2b_ref, xbuf_ref, ybuf_ref, in_sem, out_sem):
    e = pl.program_id(0)
    n_blk = n_blk_ref[e]
    block_rows = MOE_BLOCK

    def rows_of(ref, j):
        return ref.at[pl.ds(pl.multiple_of((first_blk_ref[e] + j) * block_rows, block_rows), block_rows)]

    def in_copy(j, slot):
        return pltpu.make_async_copy(rows_of(xs_ref, j), xbuf_ref.at[slot], in_sem.at[slot])

    def out_copy(j, slot):
        return pltpu.make_async_copy(ybuf_ref.at[slot], rows_of(ys_ref, j), out_sem.at[slot])

    @pl.when(n_blk > 0)
    def _():
        in_copy(0, 0).start(priority=_ROW_DMA_PRIORITY)
        w1b_ref[...] = w1_ref[0, 0].astype(BF16)
        w2b_ref[...] = w2_ref[0, 0].astype(BF16)

    def block(j, c):
        slot = j % 2

        @pl.when(j + 1 < n_blk)
        def _():
            in_copy(j + 1, 1 - slot).start(priority=_ROW_DMA_PRIORITY)

        in_copy(j, slot).wait()

        @pl.when(j >= 2)
        def _():
            out_copy(j - 2, slot).wait()

        f = w2b_ref.shape[0]
        x = xbuf_ref[slot]
        h = jnp.dot(x, w1b_ref[...], preferred_element_type=F32) + b1_ref[0, 0]
        glu = jnp.minimum(h[:, :f], SWIGLU_LIMIT)
        lin = jnp.clip(h[:, f:], -SWIGLU_LIMIT, SWIGLU_LIMIT)
        act = glu / (1.0 + jnp.exp(-SWIGLU_ALPHA * glu)) * (lin + 1.0)
        y = jnp.dot(act.astype(BF16), w2b_ref[...], preferred_element_type=F32) + b2_ref[0, 0]
        ybuf_ref[slot] = y.astype(BF16)
        out_copy(j, slot).start(priority=_ROW_DMA_PRIORITY)
        return c

    lax.fori_loop(0, n_blk, block, 0)

    @pl.when(n_blk >= 2)
    def _():
        out_copy(n_blk - 2, n_blk % 2).wait()

    @pl.when(n_blk >= 1)
    def _():
        out_copy(n_blk - 1, (n_blk - 1) % 2).wait()


def _expert_ffn(first_blk, n_blk, xs, layer, w1, b1, w2, b2):
    depth, n_exp, d, f2 = w1.shape
    f = w2.shape[2]
    by_expert = lambda e, fb, nb: (layer, e, 0, 0)
    return pl.pallas_call(
        _ffn_kernel,
        grid_spec=pltpu.PrefetchScalarGridSpec(
            num_scalar_prefetch=2,
            grid=(n_exp,),
            in_specs=[pl.BlockSpec(memory_space=pl.ANY),
                      pl.BlockSpec((1, 1, d, f2), by_expert), pl.BlockSpec((1, 1, 1, f2), by_expert),
                      pl.BlockSpec((1, 1, f, d), by_expert), pl.BlockSpec((1, 1, 1, d), by_expert)],
            out_specs=pl.BlockSpec(memory_space=pl.ANY),
            scratch_shapes=[pltpu.VMEM((d, f2), BF16), pltpu.VMEM((f, d), BF16),
                            pltpu.VMEM((2, MOE_BLOCK, d), BF16), pltpu.VMEM((2, MOE_BLOCK, d), BF16),
                            pltpu.SemaphoreType.DMA((2,)), pltpu.SemaphoreType.DMA((2,))]),
        out_shape=jax.ShapeDtypeStruct(xs.shape, BF16),
        input_output_aliases={2: 0},
        compiler_params=_params("arbitrary"),
        name="moe_expert_ffn",
    )(first_blk, n_blk, xs, w1, b1.reshape(depth, n_exp, 1, f2), w2, b2.reshape(depth, n_exp, 1, d))


def _combine_kernel(src_ref, nch_ref, off_ref, tot_ref, ys_ref, gate_ref, eidx_ref, rank_ref, offs_ref, x1_ref,
                    g_ref, b_ref, x2_ref, x2b_ref, buf_ref, sem, *, n_exp):
    i = pl.program_id(0)
    tm = x1_ref.shape[0]
    n_buf = buf_ref.shape[0]

    @pl.when(i == 0)
    def _():
        buf_ref[...] = jnp.zeros_like(buf_ref)

    def copy(sorted_row, buffer_row):
        return pltpu.make_async_copy(_sorted_chunk(ys_ref, sorted_row), _buffer_chunk(buf_ref, buffer_row), sem)

    _for_each_chunk(i, n_exp, src_ref, nch_ref, off_ref, lambda s, b: copy(s, b).start())
    buf_col = lax.broadcasted_iota(I32, (tm, n_buf), 1).astype(F32)
    weights = jnp.zeros((tm, n_buf), F32)
    for k, p in enumerate(_buffer_positions(eidx_ref, rank_ref, offs_ref)):
        weights = jnp.where(buf_col == p, gate_ref[:, k:k + 1], weights)
    _wait_chunks(tot_ref[i], copy(0, 0))
    y = jnp.dot(weights.astype(BF16), buf_ref[...], preferred_element_type=F32)
    x2 = _layer_norm(DEEPNORM_ALPHA * x1_ref[...] + y, g_ref[...], b_ref[...])
    x2_ref[...] = x2
    x2b_ref[...] = x2.astype(BF16)


def _combine(tables, ys, gates, eidx, rank, offs, x1, ln_g, ln_b):
    t, d = x1.shape
    n_exp = N_EXPERTS
    tm = min(t, _MOE_TILE)
    row = lambda i, *_: (i, 0)
    fixed = lambda i, *_: (0, 0)
    return pl.pallas_call(
        functools.partial(_combine_kernel, n_exp=n_exp),
        grid_spec=pltpu.PrefetchScalarGridSpec(
            num_scalar_prefetch=4,
            grid=(t // tm,),
            in_specs=[pl.BlockSpec(memory_space=pl.ANY), pl.BlockSpec((tm, _LANES), row),
                      pl.BlockSpec((tm, _LANES), row), pl.BlockSpec((tm, _LANES), row),
                      pl.BlockSpec((1, _SUBLANES, _LANES), lambda i, *_: (i, 0, 0)),
                      pl.BlockSpec((tm, d), row), pl.BlockSpec((1, d), fixed), pl.BlockSpec((1, d), fixed)],
            out_specs=[pl.BlockSpec((tm, d), row), pl.BlockSpec((tm, d), row)],
            scratch_shapes=[pltpu.VMEM((_run_rows(tm, n_exp), d), BF16),
                            pltpu.SemaphoreType.DMA(())]),
        out_shape=[jax.ShapeDtypeStruct((t, d), F32), jax.ShapeDtypeStruct((t, d), BF16)],
        compiler_params=_params("arbitrary"),
        name="moe_combine",
    )(*tables, ys, gates, eidx, rank, offs, x1, ln_g.reshape(1, d), ln_b.reshape(1, d))


def _moe_layer(x1, x1b, eidx, gates, rank, starts, counts, layer, w1, b1, w2, b2, ln_g, ln_b):
    t = x1.shape[0]
    n_exp = w1.shape[1]
    slack = _RUN_CHUNK - 1
    n_tiles = starts.shape[0]
    n_blocks = -(-(t * TOP_K + n_exp * (n_tiles * (_SUBLANES - 1) + slack)) // MOE_BLOCK) + n_exp
    cnt = counts[0, :n_exp]
    picks_before = starts[:, 0, :n_exp]
    run_len = jnp.concatenate([picks_before[1:], cnt[None]], axis=0) - picks_before
    run_rows = -(-run_len // _SUBLANES) * _SUBLANES
    run_start = jnp.cumsum(run_rows, axis=0) - run_rows
    used = jnp.sum(run_rows, axis=0)
    padded = jnp.where(used > 0, ((used + slack + MOE_BLOCK - 1) // MOE_BLOCK) * MOE_BLOCK, 0)
    pad_ends = jnp.cumsum(padded)
    pad_starts = pad_ends - padded
    n_chunks = (run_len + slack) // _RUN_CHUNK
    buf_off = (jnp.cumsum(n_chunks, axis=1) - n_chunks) * _RUN_CHUNK
    src = pad_starts[None, :] + run_start
    tables = (src.astype(I32).reshape(-1), n_chunks.astype(I32).reshape(-1),
              buf_off.astype(I32).reshape(-1), jnp.sum(n_chunks, axis=1).astype(I32))
    offs = jnp.zeros(starts.shape, I32).at[:, :, :n_exp].set(buf_off[:, None, :].astype(I32))
    tail = pad_ends - (src[-1] + n_chunks[-1] * _RUN_CHUNK)
    z_n = jnp.concatenate([(tail + slack) // _RUN_CHUNK, (n_blocks * MOE_BLOCK - pad_ends[-1:]) // _RUN_CHUNK])
    z_src = jnp.concatenate([pad_ends, pad_ends[-1:]]) - z_n * _RUN_CHUNK
    z_src = z_src.at[-1].set(pad_ends[-1])
    zero_tables = (z_src.astype(I32), z_n.astype(I32), jnp.sum(z_n, keepdims=True).astype(I32))
    xs = _dispatch(tables, zero_tables, x1b, eidx, rank, offs, n_blocks * MOE_BLOCK)
    ys = _expert_ffn((pad_starts // MOE_BLOCK).astype(I32), (padded // MOE_BLOCK).astype(I32), xs,
                     layer, w1, b1, w2, b2)
    return _combine(tables, ys, gates, eidx, rank, offs, x1, ln_g, ln_b)


def _fox_mixer(x2, xb, w_in, b_f, batch, seq):
    d = x2.shape[1]
    qkv = _matmul(xb, w_in[:, :3 * d].astype(BF16), BF16)
    c_pad = _fox_gate_cumsum(x2, w_in[:, 3 * d:], b_f, batch, seq)
    return _fox_attention(qkv, c_pad, batch, seq, FOX_HEADS)


def _moba_mixer(xb, w_in, batch, seq):
    return _moba_attention(_matmul(xb, w_in.astype(BF16), BF16), batch, seq, MOBA_HEADS)


def _retention_mixer(xb, w_in, gn_g, batch, seq):
    return _retention(_matmul(xb, w_in.astype(BF16), BF16), gn_g, batch, seq, RET_HEADS)


def kernel(x, fox_w_in, fox_b_f, fox_w_out, moba_w_in, moba_w_out, ret_w_in, ret_gn_g, ret_w_out, ln_g, ln_b,
           router_w, router_b, moe_w1, moe_b1, moe_w2, moe_b2):
    batch, seq, d = x.shape
    x2 = x.reshape(batch * seq, d)
    xb = x2.astype(BF16)
    for i in range(DEPTH):
        kind, j = i % 3, i // 3
        if kind == 0:
            o, w_out = _fox_mixer(x2, xb, fox_w_in[j], fox_b_f[j], batch, seq), fox_w_out[j]
        elif kind == 1:
            o, w_out = _moba_mixer(xb, moba_w_in[j], batch, seq), moba_w_out[j]
        else:
            o, w_out = _retention_mixer(xb, ret_w_in[j], ret_gn_g[j], batch, seq), ret_w_out[j]
        routed = _post_mixer(o, w_out.astype(BF16), x2, ln_g[i, 0], ln_b[i, 0], router_w[i], router_b[i])
        x2, xb = _moe_layer(*routed, i, moe_w1, moe_b1, moe_w2, moe_b2, ln_g[i, 1], ln_b[i, 1])
    return x2.reshape(batch, seq, d)
```

```python
import functools

import jax
import jax.numpy as jnp
from jax import lax
from jax.experimental import pallas as pl
from jax.experimental.pallas import tpu as pltpu

F32, BF16, I32 = jnp.float32, jnp.bfloat16, jnp.int32
_HIGHEST = lax.Precision.HIGHEST
_NT = (((1,), (1,)), ((), ()))
_LANES = 128
_SUBLANES = 8
_BF16_ROWS = 16
_MASKED = -1e30
_BIAS_OFF = -32768.0
_LOG2E = 1.4426950408889634
_VMEM_LIMIT = 56 * 1024 * 1024
_ATTN_TILE = 1024
_KV_CHUNK = 1024
_MOE_TILE = 512
_RUN_CHUNK = 16
_ROW_DMA_PRIORITY = 1

DEPTH = 4
FOX_HEADS, MOBA_HEADS, RET_HEADS = 8, 8, 4
MOBA_BLOCK, MOBA_TOPK = 256, 3
RET_ROPE_BASE, RET_GN_EPS = 10000.0, 1e-6
N_EXPERTS, TOP_K, MOE_BLOCK = 32, 4, 256
SWIGLU_LIMIT, SWIGLU_ALPHA = 7.0, 1.702
LN_EPS = 1e-5
DEEPNORM_ALPHA = (2 * DEPTH) ** 0.25


def _params(*sem):
    return pltpu.CompilerParams(dimension_semantics=sem, vmem_limit_bytes=_VMEM_LIMIT)


def _layer_norm(y, g, b):
    mu = jnp.mean(y, axis=-1, keepdims=True)
    d = y - mu
    var = jnp.mean(d * d, axis=-1, keepdims=True)
    return d * lax.rsqrt(var + LN_EPS) * g + b


def _mm_kernel(a_ref, w_ref, o_ref):
    o_ref[...] = jnp.dot(a_ref[...], w_ref[...], preferred_element_type=F32).astype(o_ref.dtype)


def _matmul(a, w, out_dtype):
    m, k = a.shape
    n = w.shape[1]
    tm = min(m, 2048)
    tn = 1536 if n % 1536 == 0 else min(n, 1024)
    return pl.pallas_call(
        _mm_kernel,
        grid=(m // tm, n // tn),
        in_specs=[pl.BlockSpec((tm, k), lambda i, j: (i, 0)),
                  pl.BlockSpec((k, tn), lambda i, j: (0, j))],
        out_specs=pl.BlockSpec((tm, tn), lambda i, j: (i, j)),
        out_shape=jax.ShapeDtypeStruct((m, n), out_dtype),
        compiler_params=_params("parallel", "parallel"),
        name="proj_matmul",
    )(a, w)


def _flash_first(s, v):
    m = jnp.max(s, axis=1, keepdims=True)
    p = jnp.exp2(s - m)
    l = jnp.sum(p, axis=1, keepdims=True)
    acc = jnp.dot(p.astype(BF16), v, preferred_element_type=F32)
    return m, l, acc


def _flash_next(carry, s, v):
    m, l, acc = carry
    m_new = jnp.maximum(m, jnp.max(s, axis=1, keepdims=True))
    a = jnp.exp2(m - m_new)
    p = jnp.exp2(s - m_new)
    l = a * l + jnp.sum(p, axis=1, keepdims=True)
    acc = a * acc + jnp.dot(p.astype(BF16), v, preferred_element_type=F32)
    return m_new, l, acc


def _causal_flash(q_aug, kaug_ref, v_ref, i, tq):
    def scores(j, width):
        start = pl.multiple_of(j * tq, tq)
        return lax.dot_general(q_aug, kaug_ref[pl.ds(start, width), :], _NT, preferred_element_type=F32)

    def values(j, width):
        return v_ref[pl.ds(pl.multiple_of(j * tq, tq), width), :]

    row = lax.broadcasted_iota(I32, (tq, tq), 0)
    col = lax.broadcasted_iota(I32, (tq, tq), 1)
    carry = _flash_first(jnp.where(col <= row, scores(i, tq), _MASKED), values(i, tq))
    group = max(1, min(_KV_CHUNK, kaug_ref.shape[0]) // tq)
    carry = lax.fori_loop(
        0, i // group, lambda p, c: _flash_next(c, scores(group * p, group * tq), values(group * p, group * tq)),
        carry)
    if group == 2:
        carry = lax.cond(i % 2 == 1, lambda c: _flash_next(c, scores(i - 1, tq), values(i - 1, tq)),
                         lambda c: c, carry)
    _, l, acc = carry
    return acc / l


def _lane_columns(n, cols):
    lane = lax.broadcasted_iota(I32, (n, _LANES), 1)
    out = jnp.zeros((n, _LANES), F32)
    for t, c in enumerate(cols):
        out = jnp.where(lane == t, c, out)
    return out.astype(BF16)


def _split3(c):
    hi = c.astype(BF16).astype(F32)
    r = c - hi
    mid = r.astype(BF16).astype(F32)
    lo = (r - mid).astype(BF16).astype(F32)
    return [hi, mid, lo]


def _fox_gate_kernel(x_ref, wf_ref, bf_ref, c_ref, carry_ref):
    @pl.when(pl.program_id(1) == 0)
    def _():
        carry_ref[...] = jnp.zeros_like(carry_ref)

    x = x_ref[...]
    x_hi = x.astype(BF16)
    x_lo = (x - x_hi.astype(F32)).astype(BF16)
    hi_part = jnp.dot(x_hi, wf_ref[...], preferred_element_type=F32)
    lo_part = jnp.dot(x_lo, wf_ref[:, :_LANES], preferred_element_type=F32)
    z = hi_part[:, :_LANES] + hi_part[:, _LANES:] + lo_part + bf_ref[...]
    log_f = jnp.minimum(z, 0.0) - jnp.log1p(jnp.exp(-jnp.abs(z)))
    tc = z.shape[0]
    row = lax.broadcasted_iota(I32, (tc, tc), 0)
    col = lax.broadcasted_iota(I32, (tc, tc), 1)
    tri = jnp.where(col <= row, 1.0, 0.0).astype(BF16)
    hi, mid, lo = [p.astype(BF16) for p in _split3(log_f)]
    hi_mid = jnp.dot(tri, jnp.concatenate([hi, mid], axis=1), preferred_element_type=F32)
    c = (hi_mid[:, :_LANES] + hi_mid[:, _LANES:] + jnp.dot(tri, lo, preferred_element_type=F32)
         + carry_ref[...])
    c_ref[...] = c
    carry_ref[...] = c[tc - 1:tc, :]


def _fox_gate_cumsum(x2, w_f, b_f, batch, seq):
    t, d = x2.shape
    h = w_f.shape[1]
    wf = jnp.zeros((d, _LANES), F32).at[:, :h].set(w_f)
    wf_hi = wf.astype(BF16)
    wf = jnp.concatenate([wf_hi, (wf - wf_hi.astype(F32)).astype(BF16)], axis=1)
    bf = jnp.zeros((1, _LANES), F32).at[0, :h].set(b_f)
    tc = min(seq, 512)
    ns = seq // tc
    return pl.pallas_call(
        _fox_gate_kernel,
        grid=(batch, ns),
        in_specs=[pl.BlockSpec((tc, d), lambda b, s: (b * ns + s, 0)),
                  pl.BlockSpec((d, 2 * _LANES), lambda b, s: (0, 0)),
                  pl.BlockSpec((1, _LANES), lambda b, s: (0, 0))],
        out_specs=pl.BlockSpec((tc, _LANES), lambda b, s: (b * ns + s, 0)),
        out_shape=jax.ShapeDtypeStruct((t, _LANES), F32),
        scratch_shapes=[pltpu.VMEM((1, _LANES), F32)],
        compiler_params=_params("parallel", "arbitrary"),
        name="fox_gate_cumsum",
    )(x2, wf, bf)


def _fox_attn_kernel(q_ref, k_ref, v_ref, cq_ref, ck_ref, o_ref, kaug_ref, *, scale):
    h = pl.program_id(1)
    i = pl.program_id(2)
    tq, hd = q_ref.shape
    seq = k_ref.shape[0]

    def head_column(c_ref, rows):
        lane = lax.broadcasted_iota(I32, (rows, _LANES), 1)
        return jnp.sum(jnp.where(lane == h, c_ref[...], 0.0), axis=1, keepdims=True) * _LOG2E

    @pl.when(i == 0)
    def _():
        kaug_ref[:, :hd] = k_ref[...]
        kaug_ref[:, hd:] = _lane_columns(seq, [1.0, 1.0, 1.0] + [-c for c in _split3(head_column(ck_ref, seq))])

    q = (q_ref[...].astype(F32) * (scale * _LOG2E)).astype(BF16)
    q_aug = jnp.concatenate([q, _lane_columns(tq, _split3(head_column(cq_ref, tq)) + [1.0, 1.0, 1.0])], axis=1)
    o_ref[...] = _causal_flash(q_aug, kaug_ref, v_ref, i, tq).astype(o_ref.dtype)


def _fox_attention(qkv, c_pad, batch, seq, heads):
    t = qkv.shape[0]
    d = qkv.shape[1] // 3
    hd = d // heads
    tq = min(seq, _ATTN_TILE)
    nq = seq // tq
    return pl.pallas_call(
        functools.partial(_fox_attn_kernel, scale=hd ** -0.5),
        grid=(batch, heads, nq),
        in_specs=[pl.BlockSpec((tq, hd), lambda b, h, i: (b * nq + i, h)),
                  pl.BlockSpec((seq, hd), lambda b, h, i: (b, heads + h)),
                  pl.BlockSpec((seq, hd), lambda b, h, i: (b, 2 * heads + h)),
                  pl.BlockSpec((tq, _LANES), lambda b, h, i: (b * nq + i, 0)),
                  pl.BlockSpec((seq, _LANES), lambda b, h, i: (b, 0))],
        out_specs=pl.BlockSpec((tq, hd), lambda b, h, i: (b * nq + i, h)),
        out_shape=jax.ShapeDtypeStruct((t, d), BF16),
        scratch_shapes=[pltpu.VMEM((seq, hd + _LANES), BF16)],
        compiler_params=_params("parallel", "parallel", "arbitrary"),
        name="fox_attention",
    )(qkv, qkv, qkv, c_pad, c_pad)


def _moba_attn_kernel(q_ref, k_ref, v_ref, o_ref, kaug_ref, kmean_ref, *, scale, blk, topk):
    i = pl.program_id(2)
    tq, hd = q_ref.shape
    seq = k_ref.shape[0]
    nb = seq // blk

    @pl.when(i == 0)
    def _():
        kmean_ref[...] = jnp.zeros_like(kmean_ref)
        for n in range(nb):
            kb = k_ref[n * blk:(n + 1) * blk, :].astype(F32)
            kmean_ref[n:n + 1, :] = jnp.mean(kb, axis=0, keepdims=True)
        kaug_ref[:, :hd] = k_ref[...]
        key_block = lax.broadcasted_iota(I32, (seq, _LANES), 0) // blk
        lane = lax.broadcasted_iota(I32, (seq, _LANES), 1)
        kaug_ref[:, hd:] = jnp.where(key_block == lane, 1.0, 0.0).astype(BF16)

    q = q_ref[...]
    gate = lax.dot_general(q.astype(F32), kmean_ref[...], _NT, precision=_HIGHEST,
                           preferred_element_type=F32)
    lane = lax.broadcasted_iota(I32, gate.shape, 1)
    lane_f = lane.astype(F32)
    own = (i * tq + lax.broadcasted_iota(I32, gate.shape, 0)) // blk
    gate = jnp.where(lane < own, gate, -jnp.inf)
    allowed = lane == own
    for r in range(topk):
        mx = jnp.max(gate, axis=1, keepdims=True)
        idx = jnp.min(jnp.where(gate == mx, lane_f, float(_LANES)), axis=1, keepdims=True)
        pick = lane_f == idx
        allowed = jnp.logical_or(allowed, jnp.logical_and(pick, r < own))
        gate = jnp.where(pick, -jnp.inf, gate)
    bias = jnp.where(allowed, 0.0, _BIAS_OFF).astype(BF16)
    q_aug = jnp.concatenate([(q.astype(F32) * (scale * _LOG2E)).astype(BF16), bias], axis=1)
    o_ref[...] = _causal_flash(q_aug, kaug_ref, v_ref, i, tq).astype(o_ref.dtype)


def _moba_attention(qkv, batch, seq, heads):
    t = qkv.shape[0]
    d = qkv.shape[1] // 3
    hd = d // heads
    blk = MOBA_BLOCK
    assert seq % blk == 0 and seq // blk <= _LANES
    tq = _ATTN_TILE if seq % _ATTN_TILE == 0 else blk
    nq = seq // tq
    return pl.pallas_call(
        functools.partial(_moba_attn_kernel, scale=hd ** -0.5, blk=blk, topk=min(MOBA_TOPK, seq // blk)),
        grid=(batch, heads, nq),
        in_specs=[pl.BlockSpec((tq, hd), lambda b, h, i: (b * nq + i, h)),
                  pl.BlockSpec((seq, hd), lambda b, h, i: (b, heads + h)),
                  pl.BlockSpec((seq, hd), lambda b, h, i: (b, 2 * heads + h))],
        out_specs=pl.BlockSpec((tq, hd), lambda b, h, i: (b * nq + i, h)),
        out_shape=jax.ShapeDtypeStruct((t, d), BF16),
        scratch_shapes=[pltpu.VMEM((seq, hd + _LANES), BF16), pltpu.VMEM((_LANES, hd), F32)],
        compiler_params=_params("parallel", "parallel", "arbitrary"),
        name="moba_attention",
    )(qkv, qkv, qkv)


def _retention_kernel(q_ref, k_ref, v_ref, g_ref, cos_ref, sin_ref, dm_ref, xi_ref, zeta_ref, gc_ref,
                      gn_ref, o_ref, state_ref, *, k_scale):
    @pl.when(pl.program_id(2) == 0)
    def _():
        state_ref[...] = jnp.zeros_like(state_ref)

    cos, sin = cos_ref[...], sin_ref[...]
    half = cos.shape[1]

    def rotate(t):
        t = t.astype(F32)
        t1, t2 = t[:, :half], t[:, half:]
        return jnp.concatenate([t1 * cos - t2 * sin, t1 * sin + t2 * cos], axis=1)

    q = rotate(q_ref[...]).astype(BF16)
    k_rot = rotate(k_ref[...]) * k_scale
    k = k_rot.astype(BF16)
    k_t = k_rot.T.astype(BF16)
    v = v_ref[...]
    s = lax.dot_general(q, k, _NT, preferred_element_type=F32) * dm_ref[0]
    inner = jnp.dot(s.astype(BF16), v, preferred_element_type=F32)
    state = state_ref[...]
    cross = jnp.dot(q, state.astype(BF16), preferred_element_type=F32) * xi_ref[0]
    o = inner + cross
    vz = (v.astype(F32) * zeta_ref[0]).astype(BF16)
    state_ref[...] = gc_ref[0][:, :1] * state + jnp.dot(k_t, vz, preferred_element_type=F32)

    mu = jnp.mean(o, axis=-1, keepdims=True)
    d = o - mu
    var = jnp.mean(d * d, axis=-1, keepdims=True)
    o = d * lax.rsqrt(var + RET_GN_EPS) * gn_ref[...]
    g = g_ref[...].astype(F32)
    o_ref[...] = (g / (1.0 + jnp.exp(-g)) * o).astype(o_ref.dtype)


def _retention(proj, gn_g, batch, seq, heads):
    t, width = proj.shape
    dk = width // (6 * heads)
    dv = 2 * dk
    chunk = min(seq, 512)
    nc = seq // chunk
    pos = jnp.arange(seq, dtype=F32)
    inv_freq = jnp.exp(-jnp.log(RET_ROPE_BASE) * jnp.arange(0, dk, 2, dtype=F32) / dk)
    ang = pos[:, None] * inv_freq[None, :]
    cos, sin = jnp.cos(ang), jnp.sin(ang)
    log_gamma = jnp.log(1.0 - jnp.exp2(-5.0 - jnp.arange(heads, dtype=F32)))
    n = jnp.arange(chunk, dtype=F32)
    diff = n[:, None] - n[None, :]
    d_mask = jnp.where(diff[None] >= 0, jnp.exp(diff[None] * log_gamma[:, None, None]), 0.0)
    xi = jnp.exp((n[None, :] + 1.0) * log_gamma[:, None])[:, :, None]
    zeta = jnp.exp((chunk - 1.0 - n[None, :]) * log_gamma[:, None])[:, :, None]
    g_chunk = jnp.broadcast_to(jnp.exp(chunk * log_gamma)[:, None, None], (heads, 1, _LANES))
    return pl.pallas_call(
        functools.partial(_retention_kernel, k_scale=dk ** -0.5),
        grid=(batch, heads, nc),
        in_specs=[pl.BlockSpec((chunk, dk), lambda b, h, c: (b * nc + c, h)),
                  pl.BlockSpec((chunk, dk), lambda b, h, c: (b * nc + c, heads + h)),
                  pl.BlockSpec((chunk, dv), lambda b, h, c: (b * nc + c, heads + h)),
                  pl.BlockSpec((chunk, dv), lambda b, h, c: (b * nc + c, 2 * heads + h)),
                  pl.BlockSpec((chunk, dk // 2), lambda b, h, c: (c, 0)),
                  pl.BlockSpec((chunk, dk // 2), lambda b, h, c: (c, 0)),
                  pl.BlockSpec((1, chunk, chunk), lambda b, h, c: (h, 0, 0)),
                  pl.BlockSpec((1, chunk, 1), lambda b, h, c: (h, 0, 0)),
                  pl.BlockSpec((1, chunk, 1), lambda b, h, c: (h, 0, 0)),
                  pl.BlockSpec((1, 1, _LANES), lambda b, h, c: (h, 0, 0)),
                  pl.BlockSpec((1, dv), lambda b, h, c: (0, h))],
        out_specs=pl.BlockSpec((chunk, dv), lambda b, h, c: (b * nc + c, h)),
        out_shape=jax.ShapeDtypeStruct((t, heads * dv), BF16),
        scratch_shapes=[pltpu.VMEM((dk, dv), F32)],
        compiler_params=_params("parallel", "parallel", "arbitrary"),
        name="retention",
    )(proj, proj, proj, proj, cos, sin, d_mask, xi, zeta, g_chunk, gn_g.reshape(1, -1))


def _post_mixer_kernel(o_ref, w_ref, x_ref, g_ref, b_ref, rw_ref, rb_ref,
                       x1_ref, x1b_ref, eidx_ref, gate_ref, rank_ref, start_ref, cnt_ref, run_ref, *, n_exp, topk):
    @pl.when(pl.program_id(0) == 0)
    def _():
        run_ref[...] = jnp.zeros_like(run_ref)

    h = jnp.dot(o_ref[...], w_ref[...], preferred_element_type=F32)
    x1 = _layer_norm(DEEPNORM_ALPHA * x_ref[...] + h, g_ref[...], b_ref[...])
    x1_ref[...] = x1
    x1b_ref[...] = x1.astype(BF16)
    start_ref[...] = jnp.broadcast_to(run_ref[...], start_ref.shape[1:]).astype(I32)[None]

    x_hi = x1.astype(BF16)
    x_lo = (x1 - x_hi.astype(F32)).astype(BF16)
    hi_part = jnp.dot(x_hi, rw_ref[...], preferred_element_type=F32)
    lo_part = jnp.dot(x_lo, rw_ref[:, :_LANES], preferred_element_type=F32)
    logits = hi_part[:, :_LANES] + hi_part[:, _LANES:] + lo_part + rb_ref[...]
    tm = logits.shape[0]
    lane = lax.broadcasted_iota(I32, logits.shape, 1).astype(F32)
    logits = jnp.where(lane < n_exp, logits, -jnp.inf)
    vals, idxs = [], []
    for _ in range(topk):
        mx = jnp.max(logits, axis=1, keepdims=True)
        idx = jnp.min(jnp.where(logits == mx, lane, float(_LANES)), axis=1, keepdims=True)
        vals.append(mx)
        idxs.append(idx)
        logits = jnp.where(lane == idx, -jnp.inf, logits)
    exps = [jnp.exp(v - vals[0]) for v in vals]
    den = exps[0]
    for e in exps[1:]:
        den = den + e

    picked = jnp.zeros(logits.shape, F32)
    for idx in idxs:
        picked = jnp.where(lane == idx, 1.0, picked)
    row = lax.broadcasted_iota(I32, (tm, tm), 0)
    col = lax.broadcasted_iota(I32, (tm, tm), 1)
    before = (col < row).astype(BF16)
    prefix = jnp.dot(before, picked.astype(BF16), preferred_element_type=F32)

    eidx = jnp.zeros(logits.shape, F32)
    gates = jnp.zeros(logits.shape, F32)
    ranks = jnp.zeros(logits.shape, F32)
    for r in range(topk):
        rank_r = jnp.sum(jnp.where(lane == idxs[r], prefix, 0.0), axis=1, keepdims=True)
        eidx = jnp.where(lane == r, idxs[r], eidx)
        gates = jnp.where(lane == r, exps[r] / den, gates)
        ranks = jnp.where(lane == r, rank_r, ranks)
    eidx_ref[...] = eidx.astype(I32)
    gate_ref[...] = gates
    rank_ref[...] = ranks.astype(I32)
    run = run_ref[...] + jnp.sum(picked, axis=0, keepdims=True)
    run_ref[...] = run
    cnt_ref[...] = jnp.broadcast_to(run, cnt_ref.shape).astype(I32)


def _post_mixer(o, w_out, x2, ln_g, ln_b, router_w, router_b):
    t, d = x2.shape
    assert d == _SUBLANES * _LANES
    kdim = o.shape[1]
    n_exp = router_w.shape[1]
    tm = min(t, _MOE_TILE)
    rw = jnp.zeros((d, _LANES), F32).at[:, :n_exp].set(router_w)
    rw_hi = rw.astype(BF16)
    rw = jnp.concatenate([rw_hi, (rw - rw_hi.astype(F32)).astype(BF16)], axis=1)
    rb = jnp.zeros((1, _LANES), F32).at[0, :n_exp].set(router_b)
    row = lambda i: (i, 0)
    fixed = lambda i: (0, 0)
    return pl.pallas_call(
        functools.partial(_post_mixer_kernel, n_exp=n_exp, topk=TOP_K),
        grid=(t // tm,),
        in_specs=[pl.BlockSpec((tm, kdim), row), pl.BlockSpec((kdim, d), fixed),
                  pl.BlockSpec((tm, d), row), pl.BlockSpec((1, d), fixed), pl.BlockSpec((1, d), fixed),
                  pl.BlockSpec((d, 2 * _LANES), fixed), pl.BlockSpec((1, _LANES), fixed)],
        out_specs=[pl.BlockSpec((tm, d), row), pl.BlockSpec((tm, d), row), pl.BlockSpec((tm, _LANES), row),
                   pl.BlockSpec((tm, _LANES), row), pl.BlockSpec((tm, _LANES), row),
                   pl.BlockSpec((1, _SUBLANES, _LANES), lambda i: (i, 0, 0)),
                   pl.BlockSpec((_SUBLANES, _LANES), fixed)],
        out_shape=[jax.ShapeDtypeStruct((t, d), F32), jax.ShapeDtypeStruct((t, d), BF16),
                   jax.ShapeDtypeStruct((t, _LANES), I32), jax.ShapeDtypeStruct((t, _LANES), F32),
                   jax.ShapeDtypeStruct((t, _LANES), I32),
                   jax.ShapeDtypeStruct((t // tm, _SUBLANES, _LANES), I32),
                   jax.ShapeDtypeStruct((_SUBLANES, _LANES), I32)],
        scratch_shapes=[pltpu.VMEM((1, _LANES), F32)],
        compiler_params=_params("arbitrary"),
        name="post_mixer",
    )(o, w_out, x2, ln_g.reshape(1, d), ln_b.reshape(1, d), rw, rb)


def _run_rows(tile_tokens, n_exp):
    rows = tile_tokens * TOP_K + n_exp * (_RUN_CHUNK - 1)
    return -(-rows // _LANES) * _LANES


def _buffer_positions(eidx_ref, rank_ref, offs_ref):
    lane = lax.broadcasted_iota(I32, eidx_ref.shape, 1)
    offs = offs_ref[0, 0:1, :].astype(F32)
    pos = []
    for k in range(TOP_K):
        start = jnp.sum(jnp.where(lane == eidx_ref[:, k:k + 1], offs, 0.0), axis=1, keepdims=True)
        pos.append(start + rank_ref[:, k:k + 1].astype(F32))
    return pos


def _for_each_chunk(i, n_exp, src_ref, nch_ref, off_ref, fn):
    def expert(e, c):
        t = i * n_exp + e

        def chunk(j, c2):
            fn(src_ref[t] + j * _RUN_CHUNK, off_ref[t] + j * _RUN_CHUNK)
            return c2

        return lax.fori_loop(0, nch_ref[t], chunk, c)

    lax.fori_loop(0, n_exp, expert, 0)


def _sorted_chunk(ref, row):
    return ref.at[pl.ds(pl.multiple_of(row, _SUBLANES), _RUN_CHUNK)]


def _buffer_chunk(ref, row):
    return ref.at[pl.ds(pl.multiple_of(row, _BF16_ROWS), _RUN_CHUNK)]


def _dispatch_kernel(src_ref, nch_ref, off_ref, tot_ref, zsrc_ref, zn_ref, ztot_ref, xb_ref, eidx_ref, rank_ref,
                     offs_ref, xs_ref, buf_ref, zero_ref, sem, *, n_exp):
    i = pl.program_id(0)
    tm = xb_ref.shape[0]
    n_buf = buf_ref.shape[0]

    @pl.when(i == 0)
    def _():
        zero_ref[...] = jnp.zeros_like(zero_ref)

        def zero_copy(sorted_row):
            return pltpu.make_async_copy(zero_ref, _sorted_chunk(xs_ref, sorted_row), sem)

        def region(e, c):
            def chunk(j, c2):
                zero_copy(zsrc_ref[e] + j * _RUN_CHUNK).start()
                return c2

            return lax.fori_loop(0, zn_ref[e], chunk, c)

        lax.fori_loop(0, n_exp + 1, region, 0)
        _wait_chunks(ztot_ref[0], zero_copy(0))

    lane = lax.broadcasted_iota(I32, eidx_ref.shape, 1)
    pos = jnp.full(eidx_ref.shape, -1.0, F32)
    for k, p in enumerate(_buffer_positions(eidx_ref, rank_ref, offs_ref)):
        pos = jnp.where(lane == k, p, pos)
    pos_t = pos.T
    buf_row = lax.broadcasted_iota(I32, (n_buf, tm), 0).astype(F32)
    place = buf_row == pos_t[0:1, :]
    for k in range(1, TOP_K):
        place = jnp.logical_or(place, buf_row == pos_t[k:k + 1, :])
    rows = jnp.dot(jnp.where(place, 1.0, 0.0).astype(BF16), xb_ref[...], preferred_element_type=F32)
    buf_ref[...] = rows.astype(BF16)

    def copy(sorted_row, buffer_row):
        return pltpu.make_async_copy(_buffer_chunk(buf_ref, buffer_row), _sorted_chunk(xs_ref, sorted_row), sem)

    _for_each_chunk(i, n_exp, src_ref, nch_ref, off_ref, lambda s, b: copy(s, b).start())
    _wait_chunks(tot_ref[i], copy(0, 0))


def _wait_chunks(n, same_size_copy):
    def wait(j, c):
        same_size_copy.wait()
        return c

    lax.fori_loop(0, n, wait, 0)


def _dispatch(tables, zero_tables, x1b, eidx, rank, offs, n_rows):
    t, d = x1b.shape
    n_exp = N_EXPERTS
    tm = min(t, _MOE_TILE)
    row = lambda i, *_: (i, 0)
    return pl.pallas_call(
        functools.partial(_dispatch_kernel, n_exp=n_exp),
        grid_spec=pltpu.PrefetchScalarGridSpec(
            num_scalar_prefetch=7,
            grid=(t // tm,),
            in_specs=[pl.BlockSpec((tm, d), row), pl.BlockSpec((tm, _LANES), row), pl.BlockSpec((tm, _LANES), row),
                      pl.BlockSpec((1, _SUBLANES, _LANES), lambda i, *_: (i, 0, 0))],
            out_specs=pl.BlockSpec(memory_space=pl.ANY),
            scratch_shapes=[pltpu.VMEM((_run_rows(tm, n_exp), d), BF16), pltpu.VMEM((_RUN_CHUNK, d), BF16),
                            pltpu.SemaphoreType.DMA(())]),
        out_shape=jax.ShapeDtypeStruct((n_rows, d), BF16),
        compiler_params=_params("arbitrary"),
        name="moe_dispatch",
    )(*tables, *zero_tables, x1b, eidx, rank, offs)


def _ffn_kernel(first_blk_ref, n_blk_ref, xs_ref, w1_ref, b1_ref, w2_ref, b2_ref, ys_ref,
                w1b_ref, w2b_ref, xbuf_ref, ybuf_ref, in_sem, out_sem):
    e = pl.program_id(0)
    n_blk = n_blk_ref[e]
    block_rows = MOE_BLOCK

    def rows_of(ref, j):
        return ref.at[pl.ds(pl.multiple_of((first_blk_ref[e] + j) * block_rows, block_rows), block_rows)]

    def in_copy(j, slot):
        return pltpu.make_async_copy(rows_of(xs_ref, j), xbuf_ref.at[slot], in_sem.at[slot])

    def out_copy(j, slot):
        return pltpu.make_async_copy(ybuf_ref.at[slot], rows_of(ys_ref, j), out_sem.at[slot])

    @pl.when(n_blk > 0)
    def _():
        in_copy(0, 0).start(priority=_ROW_DMA_PRIORITY)
        w1b_ref[...] = w1_ref[0, 0].astype(BF16)
        w2b_ref[...] = w2_ref[0, 0].astype(BF16)

    def block(j, c):
        slot = j % 2

        @pl.when(j + 1 < n_blk)
        def _():
            in_copy(j + 1, 1 - slot).start(priority=_ROW_DMA_PRIORITY)

        in_copy(j, slot).wait()

        @pl.when(j >= 2)
        def _():
            out_copy(j - 2, slot).wait()

        f = w2b_ref.shape[0]
        x = xbuf_ref[slot]
        h = jnp.dot(x, w1b_ref[...], preferred_element_type=F32) + b1_ref[0, 0]
        glu = jnp.minimum(h[:, :f], SWIGLU_LIMIT)
        lin = jnp.clip(h[:, f:], -SWIGLU_LIMIT, SWIGLU_LIMIT)
        act = glu / (1.0 + jnp.exp(-SWIGLU_ALPHA * glu)) * (lin + 1.0)
        y = jnp.dot(act.astype(BF16), w2b_ref[...], preferred_element_type=F32) + b2_ref[0, 0]
        ybuf_ref[slot] = y.astype(BF16)
        out_copy(j, slot).start(priority=_ROW_DMA_PRIORITY)
        return c

    lax.fori_loop(0, n_blk, block, 0)

    @pl.when(n_blk >= 2)
    def _():
        out_copy(n_blk - 2, n_blk % 2).wait()

    @pl.when(n_blk >= 1)
    def _():
        out_copy(n_blk - 1, (n_blk - 1) % 2).wait()


def _expert_ffn(first_blk, n_blk, xs, layer, w1, b1, w2, b2):
    depth, n_exp, d, f2 = w1.shape
    f = w2.shape[2]
    by_expert = lambda e, fb, nb: (layer, e, 0, 0)
    return pl.pallas_call(
        _ffn_kernel,
        grid_spec=pltpu.PrefetchScalarGridSpec(
            num_scalar_prefetch=2,
            grid=(n_exp,),
            in_specs=[pl.BlockSpec(memory_space=pl.ANY),
                      pl.BlockSpec((1, 1, d, f2), by_expert), pl.BlockSpec((1, 1, 1, f2), by_expert),
                      pl.BlockSpec((1, 1, f, d), by_expert), pl.BlockSpec((1, 1, 1, d), by_expert)],
            out_specs=pl.BlockSpec(memory_space=pl.ANY),
            scratch_shapes=[pltpu.VMEM((d, f2), BF16), pltpu.VMEM((f, d), BF16),
                            pltpu.VMEM((2, MOE_BLOCK, d), BF16), pltpu.VMEM((2, MOE_BLOCK, d), BF16),
                            pltpu.SemaphoreType.DMA((2,)), pltpu.SemaphoreType.DMA((2,))]),
        out_shape=jax.ShapeDtypeStruct(xs.shape, BF16),
        input_output_aliases={2: 0},
        compiler_params=_params("arbitrary"),
        name="moe_expert_ffn",
    )(first_blk, n_blk, xs, w1, b1.reshape(depth, n_exp, 1, f2), w2, b2.reshape(depth, n_exp, 1, d))


def _combine_kernel(src_ref, nch_ref, off_ref, tot_ref, ys_ref, gate_ref, eidx_ref, rank_ref, offs_ref, x1_ref,
                    g_ref, b_ref, x2_ref, x2b_ref, buf_ref, sem, *, n_exp):
    i = pl.program_id(0)
    tm = x1_ref.shape[0]
    n_buf = buf_ref.shape[0]

    @pl.when(i == 0)
    def _():
        buf_ref[...] = jnp.zeros_like(buf_ref)

    def copy(sorted_row, buffer_row):
        return pltpu.make_async_copy(_sorted_chunk(ys_ref, sorted_row), _buffer_chunk(buf_ref, buffer_row), sem)

    _for_each_chunk(i, n_exp, src_ref, nch_ref, off_ref, lambda s, b: copy(s, b).start())
    buf_col = lax.broadcasted_iota(I32, (tm, n_buf), 1).astype(F32)
    weights = jnp.zeros((tm, n_buf), F32)
    for k, p in enumerate(_buffer_positions(eidx_ref, rank_ref, offs_ref)):
        weights = jnp.where(buf_col == p, gate_ref[:, k:k + 1], weights)
    _wait_chunks(tot_ref[i], copy(0, 0))
    y = jnp.dot(weights.astype(BF16), buf_ref[...], preferred_element_type=F32)
    x2 = _layer_norm(DEEPNORM_ALPHA * x1_ref[...] + y, g_ref[...], b_ref[...])
    x2_ref[...] = x2
    x2b_ref[...] = x2.astype(BF16)


def _combine(tables, ys, gates, eidx, rank, offs, x1, ln_g, ln_b):
    t, d = x1.shape
    n_exp = N_EXPERTS
    tm = min(t, _MOE_TILE)
    row = lambda i, *_: (i, 0)
    fixed = lambda i, *_: (0, 0)
    return pl.pallas_call(
        functools.partial(_combine_kernel, n_exp=n_exp),
        grid_spec=pltpu.PrefetchScalarGridSpec(
            num_scalar_prefetch=4,
            grid=(t // tm,),
            in_specs=[pl.BlockSpec(memory_space=pl.ANY), pl.BlockSpec((tm, _LANES), row),
                      pl.BlockSpec((tm, _LANES), row), pl.BlockSpec((tm, _LANES), row),
                      pl.BlockSpec((1, _SUBLANES, _LANES), lambda i, *_: (i, 0, 0)),
                      pl.BlockSpec((tm, d), row), pl.BlockSpec((1, d), fixed), pl.BlockSpec((1, d), fixed)],
            out_specs=[pl.BlockSpec((tm, d), row), pl.BlockSpec((tm, d), row)],
            scratch_shapes=[pltpu.VMEM((_run_rows(tm, n_exp), d), BF16),
                            pltpu.SemaphoreType.DMA(())]),
        out_shape=[jax.ShapeDtypeStruct((t, d), F32), jax.ShapeDtypeStruct((t, d), BF16)],
        compiler_params=_params("arbitrary"),
        name="moe_combine",
    )(*tables, ys, gates, eidx, rank, offs, x1, ln_g.reshape(1, d), ln_b.reshape(1, d))


def _moe_layer(x1, x1b, eidx, gates, rank, starts, counts, layer, w1, b1, w2, b2, ln_g, ln_b):
    t = x1.shape[0]
    n_exp = w1.shape[1]
    slack = _RUN_CHUNK - 1
    n_tiles = starts.shape[0]
    n_blocks = -(-(t * TOP_K + n_exp * (n_tiles * (_SUBLANES - 1) + slack)) // MOE_BLOCK) + n_exp
    cnt = counts[0, :n_exp]
    picks_before = starts[:, 0, :n_exp]
    run_len = jnp.concatenate([picks_before[1:], cnt[None]], axis=0) - picks_before
    run_rows = -(-run_len // _SUBLANES) * _SUBLANES
    run_start = jnp.cumsum(run_rows, axis=0) - run_rows
    used = jnp.sum(run_rows, axis=0)
    padded = jnp.where(used > 0, ((used + slack + MOE_BLOCK - 1) // MOE_BLOCK) * MOE_BLOCK, 0)
    pad_ends = jnp.cumsum(padded)
    pad_starts = pad_ends - padded
    n_chunks = (run_len + slack) // _RUN_CHUNK
    buf_off = (jnp.cumsum(n_chunks, axis=1) - n_chunks) * _RUN_CHUNK
    src = pad_starts[None, :] + run_start
    tables = (src.astype(I32).reshape(-1), n_chunks.astype(I32).reshape(-1),
              buf_off.astype(I32).reshape(-1), jnp.sum(n_chunks, axis=1).astype(I32))
    offs = jnp.zeros(starts.shape, I32).at[:, :, :n_exp].set(buf_off[:, None, :].astype(I32))
    tail = pad_ends - (src[-1] + n_chunks[-1] * _RUN_CHUNK)
    z_n = jnp.concatenate([(tail + slack) // _RUN_CHUNK, (n_blocks * MOE_BLOCK - pad_ends[-1:]) // _RUN_CHUNK])
    z_src = jnp.concatenate([pad_ends, pad_ends[-1:]]) - z_n * _RUN_CHUNK
    z_src = z_src.at[-1].set(pad_ends[-1])
    zero_tables = (z_src.astype(I32), z_n.astype(I32), jnp.sum(z_n, keepdims=True).astype(I32))
    xs = _dispatch(tables, zero_tables, x1b, eidx, rank, offs, n_blocks * MOE_BLOCK)
    ys = _expert_ffn((pad_starts // MOE_BLOCK).astype(I32), (padded // MOE_BLOCK).astype(I32), xs,
                     layer, w1, b1, w2, b2)
    return _combine(tables, ys, gates, eidx, rank, offs, x1, ln_g, ln_b)


def _fox_mixer(x2, xb, w_in, b_f, batch, seq):
    d = x2.shape[1]
    qkv = _matmul(xb, w_in[:, :3 * d].astype(BF16), BF16)
    c_pad = _fox_gate_cumsum(x2, w_in[:, 3 * d:], b_f, batch, seq)
    return _fox_attention(qkv, c_pad, batch, seq, FOX_HEADS)


def _moba_mixer(xb, w_in, batch, seq):
    return _moba_attention(_matmul(xb, w_in.astype(BF16), BF16), batch, seq, MOBA_HEADS)


def _retention_mixer(xb, w_in, gn_g, batch, seq):
    return _retention(_matmul(xb, w_in.astype(BF16), BF16), gn_g, batch, seq, RET_HEADS)


def kernel(x, fox_w_in, fox_b_f, fox_w_out, moba_w_in, moba_w_out, ret_w_in, ret_gn_g, ret_w_out, ln_g, ln_b,
           router_w, router_b, moe_w1, moe_b1, moe_w2, moe_b2):
    batch, seq, d = x.shape
    x2 = x.reshape(batch * seq, d)
    xb = x2.astype(BF16)
    for i in range(DEPTH):
        kind, j = i % 3, i // 3
        if kind == 0:
            o, w_out = _fox_mixer(x2, xb, fox_w_in[j], fox_b_f[j], batch, seq), fox_w_out[j]
        elif kind == 1:
            o, w_out = _moba_mixer(xb, moba_w_in[j], batch, seq), moba_w_out[j]
        else:
            o, w_out = _retention_mixer(xb, ret_w_in[j], ret_gn_g[j], batch, seq), ret_w_out[j]
        routed = _post_mixer(o, w_out.astype(BF16), x2, ln_g[i, 0], ln_b[i, 0], router_w[i], router_b[i])
        x2, xb = _moe_layer(*routed, i, moe_w1, moe_b1, moe_w2, moe_b2, ln_g[i, 1], ln_b[i, 1])
    return x2.reshape(batch, seq, d)
```

```python
import functools

import jax
import jax.numpy as jnp
from jax import lax
from jax.experimental import pallas as pl
from jax.experimental.pallas import tpu as pltpu

F32, BF16, I32 = jnp.float32, jnp.bfloat16, jnp.int32
_HIGHEST = lax.Precision.HIGHEST
_NT = (((1,), (1,)), ((), ()))
_LANES = 128
_SUBLANES = 8
_BF16_ROWS = 16
_MASKED = -1e30
_BIAS_OFF = -32768.0
_LOG2E = 1.4426950408889634
_VMEM_LIMIT = 56 * 1024 * 1024
_ATTN_TILE = 1024
_KV_CHUNK = 1024
_MOE_TILE = 512
_RUN_CHUNK = 16
_ROW_DMA_PRIORITY = 1

DEPTH = 4
FOX_HEADS, MOBA_HEADS, RET_HEADS = 8, 8, 4
MOBA_BLOCK, MOBA_TOPK = 256, 3
RET_ROPE_BASE, RET_GN_EPS = 10000.0, 1e-6
N_EXPERTS, TOP_K, MOE_BLOCK = 32, 4, 256
SWIGLU_LIMIT, SWIGLU_ALPHA = 7.0, 1.702
LN_EPS = 1e-5
DEEPNORM_ALPHA = (2 * DEPTH) ** 0.25


def _params(*sem):
    return pltpu.CompilerParams(dimension_semantics=sem, vmem_limit_bytes=_VMEM_LIMIT)


def _layer_norm(y, g, b):
    mu = jnp.mean(y, axis=-1, keepdims=True)
    d = y - mu
    var = jnp.mean(d * d, axis=-1, keepdims=True)
    return d * lax.rsqrt(var + LN_EPS) * g + b


def _mm_kernel(a_ref, w_ref, o_ref):
    o_ref[...] = jnp.dot(a_ref[...], w_ref[...], preferred_element_type=F32).astype(o_ref.dtype)


def _matmul(a, w, out_dtype):
    m, k = a.shape
    n = w.shape[1]
    tm = min(m, 2048)
    tn = 1536 if n % 1536 == 0 else min(n, 1024)
    return pl.pallas_call(
        _mm_kernel,
        grid=(m // tm, n // tn),
        in_specs=[pl.BlockSpec((tm, k), lambda i, j: (i, 0)),
                  pl.BlockSpec((k, tn), lambda i, j: (0, j))],
        out_specs=pl.BlockSpec((tm, tn), lambda i, j: (i, j)),
        out_shape=jax.ShapeDtypeStruct((m, n), out_dtype),
        compiler_params=_params("parallel", "parallel"),
        name="proj_matmul",
    )(a, w)


def _flash_first(s, v):
    m = jnp.max(s, axis=1, keepdims=True)
    p = jnp.exp2(s - m)
    l = jnp.sum(p, axis=1, keepdims=True)
    acc = jnp.dot(p.astype(BF16), v, preferred_element_type=F32)
    return m, l, acc


def _flash_next(carry, s, v):
    m, l, acc = carry
    m_new = jnp.maximum(m, jnp.max(s, axis=1, keepdims=True))
    a = jnp.exp2(m - m_new)
    p = jnp.exp2(s - m_new)
    l = a * l + jnp.sum(p, axis=1, keepdims=True)
    acc = a * acc + jnp.dot(p.astype(BF16), v, preferred_element_type=F32)
    return m_new, l, acc


def _causal_flash(q_aug, kaug_ref, v_ref, i, tq):
    def scores(j, width):
        start = pl.multiple_of(j * tq, tq)
        return lax.dot_general(q_aug, kaug_ref[pl.ds(start, width), :], _NT, preferred_element_type=F32)

    def values(j, width):
        return v_ref[pl.ds(pl.multiple_of(j * tq, tq), width), :]

    row = lax.broadcasted_iota(I32, (tq, tq), 0)
    col = lax.broadcasted_iota(I32, (tq, tq), 1)
    carry = _flash_first(jnp.where(col <= row, scores(i, tq), _MASKED), values(i, tq))
    group = max(1, min(_KV_CHUNK, kaug_ref.shape[0]) // tq)
    carry = lax.fori_loop(
        0, i // group, lambda p, c: _flash_next(c, scores(group * p, group * tq), values(group * p, group * tq)),
        carry)
    if group == 2:
        carry = lax.cond(i % 2 == 1, lambda c: _flash_next(c, scores(i - 1, tq), values(i - 1, tq)),
                         lambda c: c, carry)
    _, l, acc = carry
    return acc / l


def _lane_columns(n, cols):
    lane = lax.broadcasted_iota(I32, (n, _LANES), 1)
    out = jnp.zeros((n, _LANES), F32)
    for t, c in enumerate(cols):
        out = jnp.where(lane == t, c, out)
    return out.astype(BF16)


def _split3(c):
    hi = c.astype(BF16).astype(F32)
    r = c - hi
    mid = r.astype(BF16).astype(F32)
    lo = (r - mid).astype(BF16).astype(F32)
    return [hi, mid, lo]


def _fox_gate_kernel(x_ref, wf_ref, bf_ref, c_ref, carry_ref):
    @pl.when(pl.program_id(1) == 0)
    def _():
        carry_ref[...] = jnp.zeros_like(carry_ref)

    x = x_ref[...]
    x_hi = x.astype(BF16)
    x_lo = (x - x_hi.astype(F32)).astype(BF16)
    hi_part = jnp.dot(x_hi, wf_ref[...], preferred_element_type=F32)
    lo_part = jnp.dot(x_lo, wf_ref[:, :_LANES], preferred_element_type=F32)
    z = hi_part[:, :_LANES] + hi_part[:, _LANES:] + lo_part + bf_ref[...]
    log_f = jnp.minimum(z, 0.0) - jnp.log1p(jnp.exp(-jnp.abs(z)))
    tc = z.shape[0]
    row = lax.broadcasted_iota(I32, (tc, tc), 0)
    col = lax.broadcasted_iota(I32, (tc, tc), 1)
    tri = jnp.where(col <= row, 1.0, 0.0).astype(BF16)
    hi, mid, lo = [p.astype(BF16) for p in _split3(log_f)]
    hi_mid = jnp.dot(tri, jnp.concatenate([hi, mid], axis=1), preferred_element_type=F32)
    c = (hi_mid[:, :_LANES] + hi_mid[:, _LANES:] + jnp.dot(tri, lo, preferred_element_type=F32)
         + carry_ref[...])
    c_ref[...] = c
    carry_ref[...] = c[tc - 1:tc, :]


def _fox_gate_cumsum(x2, w_f, b_f, batch, seq):
    t, d = x2.shape
    h = w_f.shape[1]
    wf = jnp.zeros((d, _LANES), F32).at[:, :h].set(w_f)
    wf_hi = wf.astype(BF16)
    wf = jnp.concatenate([wf_hi, (wf - wf_hi.astype(F32)).astype(BF16)], axis=1)
    bf = jnp.zeros((1, _LANES), F32).at[0, :h].set(b_f)
    tc = min(seq, 512)
    ns = seq // tc
    return pl.pallas_call(
        _fox_gate_kernel,
        grid=(batch, ns),
        in_specs=[pl.BlockSpec((tc, d), lambda b, s: (b * ns + s, 0)),
                  pl.BlockSpec((d, 2 * _LANES), lambda b, s: (0, 0)),
                  pl.BlockSpec((1, _LANES), lambda b, s: (0, 0))],
        out_specs=pl.BlockSpec((tc, _LANES), lambda b, s: (b * ns + s, 0)),
        out_shape=jax.ShapeDtypeStruct((t, _LANES), F32),
        scratch_shapes=[pltpu.VMEM((1, _LANES), F32)],
        compiler_params=_params("parallel", "arbitrary"),
        name="fox_gate_cumsum",
    )(x2, wf, bf)


def _fox_attn_kernel(q_ref, k_ref, v_ref, cq_ref, ck_ref, o_ref, kaug_ref, *, scale):
    h = pl.program_id(1)
    i = pl.program_id(2)
    tq, hd = q_ref.shape
    seq = k_ref.shape[0]

    def head_column(c_ref, rows):
        lane = lax.broadcasted_iota(I32, (rows, _LANES), 1)
        return jnp.sum(jnp.where(lane == h, c_ref[...], 0.0), axis=1, keepdims=True) * _LOG2E

    @pl.when(i == 0)
    def _():
        kaug_ref[:, :hd] = k_ref[...]
        kaug_ref[:, hd:] = _lane_columns(seq, [1.0, 1.0, 1.0] + [-c for c in _split3(head_column(ck_ref, seq))])

    q = (q_ref[...].astype(F32) * (scale * _LOG2E)).astype(BF16)
    q_aug = jnp.concatenate([q, _lane_columns(tq, _split3(head_column(cq_ref, tq)) + [1.0, 1.0, 1.0])], axis=1)
    o_ref[...] = _causal_flash(q_aug, kaug_ref, v_ref, i, tq).astype(o_ref.dtype)


def _fox_attention(qkv, c_pad, batch, seq, heads):
    t = qkv.shape[0]
    d = qkv.shape[1] // 3
    hd = d // heads
    tq = min(seq, _ATTN_TILE)
    nq = seq // tq
    return pl.pallas_call(
        functools.partial(_fox_attn_kernel, scale=hd ** -0.5),
        grid=(batch, heads, nq),
        in_specs=[pl.BlockSpec((tq, hd), lambda b, h, i: (b * nq + i, h)),
                  pl.BlockSpec((seq, hd), lambda b, h, i: (b, heads + h)),
                  pl.BlockSpec((seq, hd), lambda b, h, i: (b, 2 * heads + h)),
                  pl.BlockSpec((tq, _LANES), lambda b, h, i: (b * nq + i, 0)),
                  pl.BlockSpec((seq, _LANES), lambda b, h, i: (b, 0))],
        out_specs=pl.BlockSpec((tq, hd), lambda b, h, i: (b * nq + i, h)),
        out_shape=jax.ShapeDtypeStruct((t, d), BF16),
        scratch_shapes=[pltpu.VMEM((seq, hd + _LANES), BF16)],
        compiler_params=_params("parallel", "parallel", "arbitrary"),
        name="fox_attention",
    )(qkv, qkv, qkv, c_pad, c_pad)


def _moba_attn_kernel(q_ref, k_ref, v_ref, o_ref, kaug_ref, kmean_ref, *, scale, blk, topk):
    i = pl.program_id(2)
    tq, hd = q_ref.shape
    seq = k_ref.shape[0]
    nb = seq // blk

    @pl.when(i == 0)
    def _():
        kmean_ref[...] = jnp.zeros_like(kmean_ref)
        for n in range(nb):
            kb = k_ref[n * blk:(n + 1) * blk, :].astype(F32)
            kmean_ref[n:n + 1, :] = jnp.mean(kb, axis=0, keepdims=True)
        kaug_ref[:, :hd] = k_ref[...]
        key_block = lax.broadcasted_iota(I32, (seq, _LANES), 0) // blk
        lane = lax.broadcasted_iota(I32, (seq, _LANES), 1)
        kaug_ref[:, hd:] = jnp.where(key_block == lane, 1.0, 0.0).astype(BF16)

    q = q_ref[...]
    n_rows = -(-nb // _SUBLANES) * _SUBLANES
    gate = lax.dot_general(kmean_ref[0:n_rows, :], q.astype(F32), _NT, precision=_HIGHEST,
                           preferred_element_type=F32)
    block = lax.broadcasted_iota(I32, gate.shape, 0)
    block_f = block.astype(F32)
    own = (i * tq + lax.broadcasted_iota(I32, gate.shape, 1)) // blk
    gate = jnp.where(block < own, gate, -jnp.inf)
    allowed = block == own
    for r in range(topk):
        mx = jnp.max(gate, axis=0, keepdims=True)
        idx = jnp.min(jnp.where(gate == mx, block_f, float(_LANES)), axis=0, keepdims=True)
        pick = block_f == idx
        allowed = jnp.logical_or(allowed, jnp.logical_and(pick, r < own))
        gate = jnp.where(pick, -jnp.inf, gate)
    bias_t = jnp.where(allowed, 0.0, _BIAS_OFF)
    bias_t = jnp.concatenate([bias_t, jnp.zeros((_LANES - n_rows, tq), F32)], axis=0)
    bias = bias_t.T.astype(BF16)
    q_aug = jnp.concatenate([(q.astype(F32) * (scale * _LOG2E)).astype(BF16), bias], axis=1)
    o_ref[...] = _causal_flash(q_aug, kaug_ref, v_ref, i, tq).astype(o_ref.dtype)


def _moba_attention(qkv, batch, seq, heads):
    t = qkv.shape[0]
    d = qkv.shape[1] // 3
    hd = d // heads
    blk = MOBA_BLOCK
    assert seq % blk == 0 and seq // blk <= _LANES
    tq = _ATTN_TILE if seq % _ATTN_TILE == 0 else blk
    nq = seq // tq
    return pl.pallas_call(
        functools.partial(_moba_attn_kernel, scale=hd ** -0.5, blk=blk, topk=min(MOBA_TOPK, seq // blk)),
        grid=(batch, heads, nq),
        in_specs=[pl.BlockSpec((tq, hd), lambda b, h, i: (b * nq + i, h)),
                  pl.BlockSpec((seq, hd), lambda b, h, i: (b, heads + h)),
                  pl.BlockSpec((seq, hd), lambda b, h, i: (b, 2 * heads + h))],
        out_specs=pl.BlockSpec((tq, hd), lambda b, h, i: (b * nq + i, h)),
        out_shape=jax.ShapeDtypeStruct((t, d), BF16),
        scratch_shapes=[pltpu.VMEM((seq, hd + _LANES), BF16), pltpu.VMEM((_LANES, hd), F32)],
        compiler_params=_params("parallel", "parallel", "arbitrary"),
        name="moba_attention",
    )(qkv, qkv, qkv)


def _retention_kernel(q_ref, k_ref, v_ref, g_ref, cos_ref, sin_ref, dm_ref, xi_ref, zeta_ref, gc_ref,
                      gn_ref, o_ref, state_ref, *, k_scale):
    @pl.when(pl.program_id(2) == 0)
    def _():
        state_ref[...] = jnp.zeros_like(state_ref)

    cos, sin = cos_ref[...], sin_ref[...]
    half = cos.shape[1]

    def rotate(t):
        t = t.astype(F32)
        t1, t2 = t[:, :half], t[:, half:]
        return jnp.concatenate([t1 * cos - t2 * sin, t1 * sin + t2 * cos], axis=1)

    q = rotate(q_ref[...]).astype(BF16)
    k_rot = rotate(k_ref[...]) * k_scale
    k = k_rot.astype(BF16)
    k_t = k_rot.T.astype(BF16)
    v = v_ref[...]
    s = lax.dot_general(q, k, _NT, preferred_element_type=F32) * dm_ref[0]
    inner = jnp.dot(s.astype(BF16), v, preferred_element_type=F32)
    state = state_ref[...]
    cross = jnp.dot(q, state.astype(BF16), preferred_element_type=F32) * xi_ref[0]
    o = inner + cross
    vz = (v.astype(F32) * zeta_ref[0]).astype(BF16)
    state_ref[...] = gc_ref[0][:, :1] * state + jnp.dot(k_t, vz, preferred_element_type=F32)

    mu = jnp.mean(o, axis=-1, keepdims=True)
    d = o - mu
    var = jnp.mean(d * d, axis=-1, keepdims=True)
    o = d * lax.rsqrt(var + RET_GN_EPS) * gn_ref[...]
    g = g_ref[...].astype(F32)
    o_ref[...] = (g / (1.0 + jnp.exp(-g)) * o).astype(o_ref.dtype)


def _retention(proj, gn_g, batch, seq, heads):
    t, width = proj.shape
    dk = width // (6 * heads)
    dv = 2 * dk
    chunk = min(seq, 512)
    nc = seq // chunk
    pos = jnp.arange(seq, dtype=F32)
    inv_freq = jnp.exp(-jnp.log(RET_ROPE_BASE) * jnp.arange(0, dk, 2, dtype=F32) / dk)
    ang = pos[:, None] * inv_freq[None, :]
    cos, sin = jnp.cos(ang), jnp.sin(ang)
    log_gamma = jnp.log(1.0 - jnp.exp2(-5.0 - jnp.arange(heads, dtype=F32)))
    n = jnp.arange(chunk, dtype=F32)
    diff = n[:, None] - n[None, :]
    d_mask = jnp.where(diff[None] >= 0, jnp.exp(diff[None] * log_gamma[:, None, None]), 0.0)
    xi = jnp.exp((n[None, :] + 1.0) * log_gamma[:, None])[:, :, None]
    zeta = jnp.exp((chunk - 1.0 - n[None, :]) * log_gamma[:, None])[:, :, None]
    g_chunk = jnp.broadcast_to(jnp.exp(chunk * log_gamma)[:, None, None], (heads, 1, _LANES))
    return pl.pallas_call(
        functools.partial(_retention_kernel, k_scale=dk ** -0.5),
        grid=(batch, heads, nc),
        in_specs=[pl.BlockSpec((chunk, dk), lambda b, h, c: (b * nc + c, h)),
                  pl.BlockSpec((chunk, dk), lambda b, h, c: (b * nc + c, heads + h)),
                  pl.BlockSpec((chunk, dv), lambda b, h, c: (b * nc + c, heads + h)),
                  pl.BlockSpec((chunk, dv), lambda b, h, c: (b * nc + c, 2 * heads + h)),
                  pl.BlockSpec((chunk, dk // 2), lambda b, h, c: (c, 0)),
                  pl.BlockSpec((chunk, dk // 2), lambda b, h, c: (c, 0)),
                  pl.BlockSpec((1, chunk, chunk), lambda b, h, c: (h, 0, 0)),
                  pl.BlockSpec((1, chunk, 1), lambda b, h, c: (h, 0, 0)),
                  pl.BlockSpec((1, chunk, 1), lambda b, h, c: (h, 0, 0)),
                  pl.BlockSpec((1, 1, _LANES), lambda b, h, c: (h, 0, 0)),
                  pl.BlockSpec((1, dv), lambda b, h, c: (0, h))],
        out_specs=pl.BlockSpec((chunk, dv), lambda b, h, c: (b * nc + c, h)),
        out_shape=jax.ShapeDtypeStruct((t, heads * dv), BF16),
        scratch_shapes=[pltpu.VMEM((dk, dv), F32)],
        compiler_params=_params("parallel", "parallel", "arbitrary"),
        name="retention",
    )(proj, proj, proj, proj, cos, sin, d_mask, xi, zeta, g_chunk, gn_g.reshape(1, -1))


def _post_mixer_kernel(o_ref, w_ref, x_ref, g_ref, b_ref, rw_ref, rb_ref,
                       x1_ref, x1b_ref, eidx_ref, gate_ref, rank_ref, start_ref, cnt_ref, run_ref, *, n_exp, topk):
    @pl.when(pl.program_id(0) == 0)
    def _():
        run_ref[...] = jnp.zeros_like(run_ref)

    h = jnp.dot(o_ref[...], w_ref[...], preferred_element_type=F32)
    x1 = _layer_norm(DEEPNORM_ALPHA * x_ref[...] + h, g_ref[...], b_ref[...])
    x1_ref[...] = x1
    x1b_ref[...] = x1.astype(BF16)
    start_ref[...] = jnp.broadcast_to(run_ref[...], start_ref.shape[1:]).astype(I32)[None]

    x_hi = x1.astype(BF16)
    x_lo = (x1 - x_hi.astype(F32)).astype(BF16)
    hi_part = jnp.dot(x_hi, rw_ref[...], preferred_element_type=F32)
    lo_part = jnp.dot(x_lo, rw_ref[:, :_LANES], preferred_element_type=F32)
    logits = hi_part[:, :_LANES] + hi_part[:, _LANES:] + lo_part + rb_ref[...]
    tm = logits.shape[0]
    logits = logits.T[0:n_exp, :]
    expert = lax.broadcasted_iota(I32, logits.shape, 0).astype(F32)
    vals, idxs = [], []
    for _ in range(topk):
        mx = jnp.max(logits, axis=0, keepdims=True)
        idx = jnp.min(jnp.where(logits == mx, expert, float(_LANES)), axis=0, keepdims=True)
        vals.append(mx)
        idxs.append(idx)
        logits = jnp.where(expert == idx, -jnp.inf, logits)
    exps = [jnp.exp(v - vals[0]) for v in vals]
    den = exps[0]
    for e in exps[1:]:
        den = den + e

    picked = jnp.zeros(logits.shape, F32)
    for idx in idxs:
        picked = jnp.where(expert == idx, 1.0, picked)
    row = lax.broadcasted_iota(I32, (tm, tm), 0)
    col = lax.broadcasted_iota(I32, (tm, tm), 1)
    earlier = (row < col).astype(BF16)
    prefix = jnp.dot(picked.astype(BF16), earlier, preferred_element_type=F32)

    def token_major(rows):
        sub = lax.broadcasted_iota(I32, (_SUBLANES, tm), 0)
        rec = jnp.zeros((_SUBLANES, tm), F32)
        for k, r in enumerate(rows):
            rec = jnp.where(sub == k, r, rec)
        return jnp.concatenate([rec, jnp.zeros((_LANES - _SUBLANES, tm), F32)], axis=0).T

    ranks = [jnp.sum(jnp.where(expert == idx, prefix, 0.0), axis=0, keepdims=True) for idx in idxs]
    eidx_ref[...] = token_major(idxs).astype(I32)
    gate_ref[...] = token_major([e / den for e in exps])
    rank_ref[...] = token_major(ranks).astype(I32)
    counts = jnp.broadcast_to(jnp.sum(picked, axis=1, keepdims=True), (n_exp, _LANES))
    counts = jnp.concatenate([counts, jnp.zeros((_LANES - n_exp, _LANES), F32)], axis=0).T[0:1, :]
    run = run_ref[...] + counts
    run_ref[...] = run
    cnt_ref[...] = jnp.broadcast_to(run, cnt_ref.shape).astype(I32)


def _post_mixer(o, w_out, x2, ln_g, ln_b, router_w, router_b):
    t, d = x2.shape
    assert d == _SUBLANES * _LANES
    kdim = o.shape[1]
    n_exp = router_w.shape[1]
    tm = min(t, _MOE_TILE)
    rw = jnp.zeros((d, _LANES), F32).at[:, :n_exp].set(router_w)
    rw_hi = rw.astype(BF16)
    rw = jnp.concatenate([rw_hi, (rw - rw_hi.astype(F32)).astype(BF16)], axis=1)
    rb = jnp.zeros((1, _LANES), F32).at[0, :n_exp].set(router_b)
    row = lambda i: (i, 0)
    fixed = lambda i: (0, 0)
    return pl.pallas_call(
        functools.partial(_post_mixer_kernel, n_exp=n_exp, topk=TOP_K),
        grid=(t // tm,),
        in_specs=[pl.BlockSpec((tm, kdim), row), pl.BlockSpec((kdim, d), fixed),
                  pl.BlockSpec((tm, d), row), pl.BlockSpec((1, d), fixed), pl.BlockSpec((1, d), fixed),
                  pl.BlockSpec((d, 2 * _LANES), fixed), pl.BlockSpec((1, _LANES), fixed)],
        out_specs=[pl.BlockSpec((tm, d), row), pl.BlockSpec((tm, d), row), pl.BlockSpec((tm, _LANES), row),
                   pl.BlockSpec((tm, _LANES), row), pl.BlockSpec((tm, _LANES), row),
                   pl.BlockSpec((1, _SUBLANES, _LANES), lambda i: (i, 0, 0)),
                   pl.BlockSpec((_SUBLANES, _LANES), fixed)],
        out_shape=[jax.ShapeDtypeStruct((t, d), F32), jax.ShapeDtypeStruct((t, d), BF16),
                   jax.ShapeDtypeStruct((t, _LANES), I32), jax.ShapeDtypeStruct((t, _LANES), F32),
                   jax.ShapeDtypeStruct((t, _LANES), I32),
                   jax.ShapeDtypeStruct((t // tm, _SUBLANES, _LANES), I32),
                   jax.ShapeDtypeStruct((_SUBLANES, _LANES), I32)],
        scratch_shapes=[pltpu.VMEM((1, _LANES), F32)],
        compiler_params=_params("arbitrary"),
        name="post_mixer",
    )(o, w_out, x2, ln_g.reshape(1, d), ln_b.reshape(1, d), rw, rb)


def _run_rows(tile_tokens, n_exp):
    rows = tile_tokens * TOP_K + n_exp * (_RUN_CHUNK - 1)
    return -(-rows // _LANES) * _LANES


def _buffer_positions(eidx_ref, rank_ref, offs_ref):
    lane = lax.broadcasted_iota(I32, eidx_ref.shape, 1)
    offs = offs_ref[0, 0:1, :].astype(F32)
    pos = []
    for k in range(TOP_K):
        start = jnp.sum(jnp.where(lane == eidx_ref[:, k:k + 1], offs, 0.0), axis=1, keepdims=True)
        pos.append(start + rank_ref[:, k:k + 1].astype(F32))
    return pos


def _for_each_chunk(i, n_exp, src_ref, nch_ref, off_ref, fn):
    def expert(e, c):
        t = i * n_exp + e

        def chunk(j, c2):
            fn(src_ref[t] + j * _RUN_CHUNK, off_ref[t] + j * _RUN_CHUNK)
            return c2

        return lax.fori_loop(0, nch_ref[t], chunk, c)

    lax.fori_loop(0, n_exp, expert, 0)


def _sorted_chunk(ref, row):
    return ref.at[pl.ds(pl.multiple_of(row, _SUBLANES), _RUN_CHUNK)]


def _buffer_chunk(ref, row):
    return ref.at[pl.ds(pl.multiple_of(row, _BF16_ROWS), _RUN_CHUNK)]


def _dispatch_kernel(src_ref, nch_ref, off_ref, tot_ref, zsrc_ref, zn_ref, ztot_ref, xb_ref, eidx_ref, rank_ref,
                     offs_ref, xs_ref, buf_ref, zero_ref, sem, *, n_exp):
    i = pl.program_id(0)
    tm = xb_ref.shape[0]
    n_buf = buf_ref.shape[0]

    @pl.when(i == 0)
    def _():
        zero_ref[...] = jnp.zeros_like(zero_ref)

        def zero_copy(sorted_row):
            return pltpu.make_async_copy(zero_ref, _sorted_chunk(xs_ref, sorted_row), sem)

        def region(e, c):
            def chunk(j, c2):
                zero_copy(zsrc_ref[e] + j * _RUN_CHUNK).start()
                return c2

            return lax.fori_loop(0, zn_ref[e], chunk, c)

        lax.fori_loop(0, n_exp + 1, region, 0)
        _wait_chunks(ztot_ref[0], zero_copy(0))

    lane = lax.broadcasted_iota(I32, eidx_ref.shape, 1)
    pos = jnp.full(eidx_ref.shape, -1.0, F32)
    for k, p in enumerate(_buffer_positions(eidx_ref, rank_ref, offs_ref)):
        pos = jnp.where(lane == k, p, pos)
    pos_t = pos.T
    buf_row = lax.broadcasted_iota(I32, (n_buf, tm), 0).astype(F32)
    place = buf_row == pos_t[0:1, :]
    for k in range(1, TOP_K):
        place = jnp.logical_or(place, buf_row == pos_t[k:k + 1, :])
    rows = jnp.dot(jnp.where(place, 1.0, 0.0).astype(BF16), xb_ref[...], preferred_element_type=F32)
    buf_ref[...] = rows.astype(BF16)

    def copy(sorted_row, buffer_row):
        return pltpu.make_async_copy(_buffer_chunk(buf_ref, buffer_row), _sorted_chunk(xs_ref, sorted_row), sem)

    _for_each_chunk(i, n_exp, src_ref, nch_ref, off_ref, lambda s, b: copy(s, b).start())
    _wait_chunks(tot_ref[i], copy(0, 0))


def _wait_chunks(n, same_size_copy):
    def wait(j, c):
        same_size_copy.wait()
        return c

    lax.fori_loop(0, n, wait, 0)


def _dispatch(tables, zero_tables, x1b, eidx, rank, offs, n_rows):
    t, d = x1b.shape
    n_exp = N_EXPERTS
    tm = min(t, _MOE_TILE)
    row = lambda i, *_: (i, 0)
    return pl.pallas_call(
        functools.partial(_dispatch_kernel, n_exp=n_exp),
        grid_spec=pltpu.PrefetchScalarGridSpec(
            num_scalar_prefetch=7,
            grid=(t // tm,),
            in_specs=[pl.BlockSpec((tm, d), row), pl.BlockSpec((tm, _LANES), row), pl.BlockSpec((tm, _LANES), row),
                      pl.BlockSpec((1, _SUBLANES, _LANES), lambda i, *_: (i, 0, 0))],
            out_specs=pl.BlockSpec(memory_space=pl.ANY),
            scratch_shapes=[pltpu.VMEM((_run_rows(tm, n_exp), d), BF16), pltpu.VMEM((_RUN_CHUNK, d), BF16),
                            pltpu.SemaphoreType.DMA(())]),
        out_shape=jax.ShapeDtypeStruct((n_rows, d), BF16),
        compiler_params=_params("arbitrary"),
        name="moe_dispatch",
    )(*tables, *zero_tables, x1b, eidx, rank, offs)


def _ffn_kernel(first_blk_ref, n_blk_ref, xs_ref, w1_ref, b1_ref, w2_ref, b2_ref, ys_ref,
                w1b_ref, w2b_ref, xbuf_ref, ybuf_ref, in_sem, out_sem):
    e = pl.program_id(0)
    n_blk = n_blk_ref[e]
    block_rows = MOE_BLOCK

    def rows_of(ref, j):
        return ref.at[pl.ds(pl.multiple_of((first_blk_ref[e] + j) * block_rows, block_rows), block_rows)]

    def in_copy(j, slot):
        return pltpu.make_async_copy(rows_of(xs_ref, j), xbuf_ref.at[slot], in_sem.at[slot])

    def out_copy(j, slot):
        return pltpu.make_async_copy(ybuf_ref.at[slot], rows_of(ys_ref, j), out_sem.at[slot])

    @pl.when(n_blk > 0)
    def _():
        in_copy(0, 0).start(priority=_ROW_DMA_PRIORITY)
        w1b_ref[...] = w1_ref[0, 0].astype(BF16)
        w2b_ref[...] = w2_ref[0, 0].astype(BF16)

    def block(j, c):
        slot = j % 2

        @pl.when(j + 1 < n_blk)
        def _():
            in_copy(j + 1, 1 - slot).start(priority=_ROW_DMA_PRIORITY)

        in_copy(j, slot).wait()

        @pl.when(j >= 2)
        def _():
            out_copy(j - 2, slot).wait()

        f = w2b_ref.shape[0]
        x = xbuf_ref[slot]
        h = jnp.dot(x, w1b_ref[...], preferred_element_type=F32) + b1_ref[0, 0]
        glu = jnp.minimum(h[:, :f], SWIGLU_LIMIT)
        lin = jnp.clip(h[:, f:], -SWIGLU_LIMIT, SWIGLU_LIMIT)
        act = glu / (1.0 + jnp.exp(-SWIGLU_ALPHA * glu)) * (lin + 1.0)
        y = jnp.dot(act.astype(BF16), w2b_ref[...], preferred_element_type=F32) + b2_ref[0, 0]
        ybuf_ref[slot] = y.astype(BF16)
        out_copy(j, slot).start(priority=_ROW_DMA_PRIORITY)
        return c

    lax.fori_loop(0, n_blk, block, 0)

    @pl.when(n_blk >= 2)
    def _():
        out_copy(n_blk - 2, n_blk % 2).wait()

    @pl.when(n_blk >= 1)
    def _():
        out_copy(n_blk - 1, (n_blk - 1) % 2).wait()


def _expert_ffn(first_blk, n_blk, xs, layer, w1, b1, w2, b2):
    depth, n_exp, d, f2 = w1.shape
    f = w2.shape[2]
    by_expert = lambda e, fb, nb: (layer, e, 0, 0)
    return pl.pallas_call(
        _ffn_kernel,
        grid_spec=pltpu.PrefetchScalarGridSpec(
            num_scalar_prefetch=2,
            grid=(n_exp,),
            in_specs=[pl.BlockSpec(memory_space=pl.ANY),
                      pl.BlockSpec((1, 1, d, f2), by_expert), pl.BlockSpec((1, 1, 1, f2), by_expert),
                      pl.BlockSpec((1, 1, f, d), by_expert), pl.BlockSpec((1, 1, 1, d), by_expert)],
            out_specs=pl.BlockSpec(memory_space=pl.ANY),
            scratch_shapes=[pltpu.VMEM((d, f2), BF16), pltpu.VMEM((f, d), BF16),
                            pltpu.VMEM((2, MOE_BLOCK, d), BF16), pltpu.VMEM((2, MOE_BLOCK, d), BF16),
                            pltpu.SemaphoreType.DMA((2,)), pltpu.SemaphoreType.DMA((2,))]),
        out_shape=jax.ShapeDtypeStruct(xs.shape, BF16),
        input_output_aliases={2: 0},
        compiler_params=_params("arbitrary"),
        name="moe_expert_ffn",
    )(first_blk, n_blk, xs, w1, b1.reshape(depth, n_exp, 1, f2), w2, b2.reshape(depth, n_exp, 1, d))


def _combine_kernel(src_ref, nch_ref, off_ref, tot_ref, ys_ref, gate_ref, eidx_ref, rank_ref, offs_ref, x1_ref,
                    g_ref, b_ref, x2_ref, x2b_ref, buf_ref, sem, *, n_exp):
    i = pl.program_id(0)
    tm = x1_ref.shape[0]
    n_buf = buf_ref.shape[0]

    @pl.when(i == 0)
    def _():
        buf_ref[...] = jnp.zeros_like(buf_ref)

    def copy(sorted_row, buffer_row):
        return pltpu.make_async_copy(_sorted_chunk(ys_ref, sorted_row), _buffer_chunk(buf_ref, buffer_row), sem)

    _for_each_chunk(i, n_exp, src_ref, nch_ref, off_ref, lambda s, b: copy(s, b).start())
    buf_col = lax.broadcasted_iota(I32, (tm, n_buf), 1).astype(F32)
    weights = jnp.zeros((tm, n_buf), F32)
    for k, p in enumerate(_buffer_positions(eidx_ref, rank_ref, offs_ref)):
        weights = jnp.where(buf_col == p, gate_ref[:, k:k + 1], weights)
    _wait_chunks(tot_ref[i], copy(0, 0))
    y = jnp.dot(weights.astype(BF16), buf_ref[...], preferred_element_type=F32)
    x2 = _layer_norm(DEEPNORM_ALPHA * x1_ref[...] + y, g_ref[...], b_ref[...])
    x2_ref[...] = x2
    x2b_ref[...] = x2.astype(BF16)


def _combine(tables, ys, gates, eidx, rank, offs, x1, ln_g, ln_b):
    t, d = x1.shape
    n_exp = N_EXPERTS
    tm = min(t, _MOE_TILE)
    row = lambda i, *_: (i, 0)
    fixed = lambda i, *_: (0, 0)
    return pl.pallas_call(
        functools.partial(_combine_kernel, n_exp=n_exp),
        grid_spec=pltpu.PrefetchScalarGridSpec(
            num_scalar_prefetch=4,
            grid=(t // tm,),
            in_specs=[pl.BlockSpec(memory_space=pl.ANY), pl.BlockSpec((tm, _LANES), row),
                      pl.BlockSpec((tm, _LANES), row), pl.BlockSpec((tm, _LANES), row),
                      pl.BlockSpec((1, _SUBLANES, _LANES), lambda i, *_: (i, 0, 0)),
                      pl.BlockSpec((tm, d), row), pl.BlockSpec((1, d), fixed), pl.BlockSpec((1, d), fixed)],
            out_specs=[pl.BlockSpec((tm, d), row), pl.BlockSpec((tm, d), row)],
            scratch_shapes=[pltpu.VMEM((_run_rows(tm, n_exp), d), BF16),
                            pltpu.SemaphoreType.DMA(())]),
        out_shape=[jax.ShapeDtypeStruct((t, d), F32), jax.ShapeDtypeStruct((t, d), BF16)],
        compiler_params=_params("arbitrary"),
        name="moe_combine",
    )(*tables, ys, gates, eidx, rank, offs, x1, ln_g.reshape(1, d), ln_b.reshape(1, d))


def _moe_layer(x1, x1b, eidx, gates, rank, starts, counts, layer, w1, b1, w2, b2, ln_g, ln_b):
    t = x1.shape[0]
    n_exp = w1.shape[1]
    slack = _RUN_CHUNK - 1
    n_tiles = starts.shape[0]
    n_blocks = -(-(t * TOP_K + n_exp * (n_tiles * (_SUBLANES - 1) + slack)) // MOE_BLOCK) + n_exp
    cnt = counts[0, :n_exp]
    picks_before = starts[:, 0, :n_exp]
    run_len = jnp.concatenate([picks_before[1:], cnt[None]], axis=0) - picks_before
    run_rows = -(-run_len // _SUBLANES) * _SUBLANES
    run_start = jnp.cumsum(run_rows, axis=0) - run_rows
    used = jnp.sum(run_rows, axis=0)
    padded = jnp.where(used > 0, ((used + slack + MOE_BLOCK - 1) // MOE_BLOCK) * MOE_BLOCK, 0)
    pad_ends = jnp.cumsum(padded)
    pad_starts = pad_ends - padded
    n_chunks = (run_len + slack) // _RUN_CHUNK
    buf_off = (jnp.cumsum(n_chunks, axis=1) - n_chunks) * _RUN_CHUNK
    src = pad_starts[None, :] + run_start
    tables = (src.astype(I32).reshape(-1), n_chunks.astype(I32).reshape(-1),
              buf_off.astype(I32).reshape(-1), jnp.sum(n_chunks, axis=1).astype(I32))
    offs = jnp.zeros(starts.shape, I32).at[:, :, :n_exp].set(buf_off[:, None, :].astype(I32))
    tail = pad_ends - (src[-1] + n_chunks[-1] * _RUN_CHUNK)
    z_n = jnp.concatenate([(tail + slack) // _RUN_CHUNK, (n_blocks * MOE_BLOCK - pad_ends[-1:]) // _RUN_CHUNK])
    z_src = jnp.concatenate([pad_ends, pad_ends[-1:]]) - z_n * _RUN_CHUNK
    z_src = z_src.at[-1].set(pad_ends[-1])
    zero_tables = (z_src.astype(I32), z_n.astype(I32), jnp.sum(z_n, keepdims=True).astype(I32))
    xs = _dispatch(tables, zero_tables, x1b, eidx, rank, offs, n_blocks * MOE_BLOCK)
    ys = _expert_ffn((pad_starts // MOE_BLOCK).astype(I32), (padded // MOE_BLOCK).astype(I32), xs,
                     layer, w1, b1, w2, b2)
    return _combine(tables, ys, gates, eidx, rank, offs, x1, ln_g, ln_b)


def _fox_mixer(x2, xb, w_in, b_f, batch, seq):
    d = x2.shape[1]
    qkv = _matmul(xb, w_in[:, :3 * d].astype(BF16), BF16)
    c_pad = _fox_gate_cumsum(x2, w_in[:, 3 * d:], b_f, batch, seq)
    return _fox_attention(qkv, c_pad, batch, seq, FOX_HEADS)


def _moba_mixer(xb, w_in, batch, seq):
    return _moba_attention(_matmul(xb, w_in.astype(BF16), BF16), batch, seq, MOBA_HEADS)


def _retention_mixer(xb, w_in, gn_g, batch, seq):
    return _retention(_matmul(xb, w_in.astype(BF16), BF16), gn_g, batch, seq, RET_HEADS)


def kernel(x, fox_w_in, fox_b_f, fox_w_out, moba_w_in, moba_w_out, ret_w_in, ret_gn_g, ret_w_out, ln_g, ln_b,
           router_w, router_b, moe_w1, moe_b1, moe_w2, moe_b2):
    batch, seq, d = x.shape
    x2 = x.reshape(batch * seq, d)
    xb = x2.astype(BF16)
    for i in range(DEPTH):
        kind, j = i % 3, i // 3
        if kind == 0:
            o, w_out = _fox_mixer(x2, xb, fox_w_in[j], fox_b_f[j], batch, seq), fox_w_out[j]
        elif kind == 1:
            o, w_out = _moba_mixer(xb, moba_w_in[j], batch, seq), moba_w_out[j]
        else:
            o, w_out = _retention_mixer(xb, ret_w_in[j], ret_gn_g[j], batch, seq), ret_w_out[j]
        routed = _post_mixer(o, w_out.astype(BF16), x2, ln_g[i, 0], ln_b[i, 0], router_w[i], router_b[i])
        x2, xb = _moe_layer(*routed, i, moe_w1, moe_b1, moe_w2, moe_b2, ln_g[i, 1], ln_b[i, 1])
    return x2.reshape(batch, seq, d)
```
